```python
import jax, jax.numpy as jnp
from jax import lax
import numpy as np

D_MODEL = 1024
BATCH = 2
SEQ = 8192
DEPTH = 2

N_EVEN = (DEPTH + 1) // 2
N_ODD = DEPTH // 2
D_PLE = 256
EPS = 1e-6

CONV_GROUPS = 8
CONV_GROUP_DIM = 64
CONV_DIM = CONV_GROUPS * CONV_GROUP_DIM
CONV_WIDTH = 3
GMLP_HEADS = 8
GMLP_HEAD_DIM = 64
GMLP_DIM = GMLP_HEADS * GMLP_HEAD_DIM
GMLP_CHUNK = 128
EVEN_IN = 3 * CONV_DIM + 2 * GMLP_DIM
EVEN_MIX = CONV_DIM + GMLP_DIM

MLA_HEADS = 8
MLA_NOPE = 64
MLA_ROPE = 32
MLA_V = 64
MLA_QK = MLA_NOPE + MLA_ROPE
Q_LORA = 384
KV_LORA = 256
ROPE_THETA = 10000.0
ATTN_BLOCK = 128
MLSTM_HEADS = 4
MLSTM_QK = 64
MLSTM_V = 128
MLSTM_CHUNK = 64
ODD_IN = Q_LORA + KV_LORA + MLA_ROPE + 2 * MLSTM_HEADS * MLSTM_QK + 2 * MLSTM_HEADS * MLSTM_V + 2 * MLSTM_HEADS
ODD_MIX = MLA_HEADS * MLA_V + MLSTM_HEADS * MLSTM_V

D_FF = -(-8 * D_MODEL // (3 * 256)) * 256

kernel_name = 'hybrid_conv_gmlp_mla_mlstm'


def rmsnorm(x, g):
    xf = x.astype(jnp.float32)
    y = xf * lax.rsqrt(jnp.mean(xf * xf, axis=-1, keepdims=True) + EPS)
    return (y * g.astype(jnp.float32)).astype(x.dtype)


def swiglu(h, w_gate, w_up, w_down):
    return (jax.nn.silu(h @ w_gate) * (h @ w_up)) @ w_down


def short_gated_conv(b_gate, c_gate, x_in, w_conv):
    z = c_gate * x_in
    conv = lax.conv_general_dilated(
        z, w_conv[:, None, :].astype(z.dtype), window_strides=(1,),
        padding=[(CONV_WIDTH - 1, 0)], dimension_numbers=('NWC', 'WIO', 'NWC'),
        feature_group_count=CONV_DIM)
    return b_gate * conv


def chunked_spatial_gating(u, v, g_v, w_s, b_s):
    bsz, seq, _ = u.shape
    n_chunks = seq // GMLP_CHUNK
    shape5 = (bsz, n_chunks, GMLP_CHUNK, GMLP_HEADS, GMLP_HEAD_DIM)
    u = jax.nn.gelu(u).reshape(shape5)
    v = rmsnorm(jax.nn.gelu(v).reshape(shape5), g_v)
    causal = jnp.tril(jnp.ones((GMLP_CHUNK, GMLP_CHUNK), dtype=w_s.dtype))
    mixed = jnp.einsum('hts,bnshd->bnthd', w_s * causal, v) + b_s.T[:, :, None]
    return (u * mixed).reshape(bsz, seq, GMLP_DIM)


def rope_tables(positions, dim, dtype):
    inv_freq = ROPE_THETA ** (-jnp.arange(0, dim, 2, dtype=jnp.float32) / dim)
    ang = positions.astype(jnp.float32)[..., None] * inv_freq
    return jnp.cos(ang)[:, :, None, :].astype(dtype), jnp.sin(ang)[:, :, None, :].astype(dtype)


def apply_rope(x, cos, sin):
    half = x.shape[-1] // 2
    x1, x2 = x[..., :half], x[..., half:]
    return jnp.concatenate([x1 * cos - x2 * sin, x2 * cos + x1 * sin], axis=-1)


def causal_block_attention(q, k, v):
    bsz, seq, heads, dk = q.shape
    n_blocks = seq // ATTN_BLOCK
    q_blocks = q.reshape(bsz, n_blocks, ATTN_BLOCK, heads, dk).transpose(1, 0, 2, 3, 4)
    k_idx = jnp.arange(seq)
    scale = dk ** -0.5

    def one_block(args):
        qi, bi = args
        s = jnp.einsum('bqhd,bkhd->bhqk', qi, k).astype(jnp.float32) * scale
        q_idx = bi * ATTN_BLOCK + jnp.arange(ATTN_BLOCK)
        s = jnp.where(k_idx[None, :] <= q_idx[:, None], s, -jnp.inf)
        pr = jax.nn.softmax(s, axis=-1).astype(v.dtype)
        return jnp.einsum('bhqk,bkhv->bqhv', pr, v)

    out = lax.map(one_block, (q_blocks, jnp.arange(n_blocks)))
    return out.transpose(1, 0, 2, 3, 4).reshape(bsz, seq, heads, v.shape[-1])


def latent_attention(q_lat, kv_lat, k_pe, positions, g_qa, g_kva, w_q_up, w_kv_up, g_q, g_k):
    bsz, seq, _ = q_lat.shape
    q = (rmsnorm(q_lat, g_qa) @ w_q_up).reshape(bsz, seq, MLA_HEADS, MLA_QK)
    kv = (rmsnorm(kv_lat, g_kva) @ w_kv_up).reshape(bsz, seq, MLA_HEADS, MLA_NOPE + MLA_V)
    k_nope, v = kv[..., :MLA_NOPE], kv[..., MLA_NOPE:]
    cos, sin = rope_tables(positions, MLA_ROPE, q.dtype)
    q_nope = rmsnorm(q[..., :MLA_NOPE], g_q[:MLA_NOPE])
    q_pe = apply_rope(rmsnorm(q[..., MLA_NOPE:], g_q[MLA_NOPE:]), cos, sin)
    k_nope = rmsnorm(k_nope, g_k[:MLA_NOPE])
    k_pe = apply_rope(rmsnorm(k_pe[:, :, None, :], g_k[MLA_NOPE:]), cos, sin)
    q = jnp.concatenate([q_nope, q_pe], axis=-1)
    k = jnp.concatenate([k_nope, jnp.broadcast_to(k_pe, (bsz, seq, MLA_HEADS, MLA_ROPE))], axis=-1)
    return causal_block_attention(q, k, v).reshape(bsz, seq, MLA_HEADS * MLA_V)


def mlstm_chunkwise(q, k, v, i_pre, f_pre):
    out_dtype = v.dtype
    bsz, seq, heads, dk = q.shape
    dv = v.shape[-1]
    L = MLSTM_CHUNK
    nc = seq // L
    f32 = jnp.float32

    def to_chunks(a):
        return a.astype(f32).reshape(bsz, nc, L, heads, -1).transpose(1, 0, 3, 2, 4)

    def gate_chunks(a):
        return a.astype(f32).reshape(bsz, nc, L, heads).transpose(1, 0, 3, 2)

    qc = to_chunks(q) * (dk ** -0.5)
    kc, vc = to_chunks(k), to_chunks(v)
    ic = gate_chunks(i_pre)
    ac = jnp.cumsum(jax.nn.log_sigmoid(gate_chunks(f_pre)), axis=-1)
    causal = jnp.tril(jnp.ones((L, L), dtype=bool))

    def step(carry, xs):
        c_mat, n_vec, m = carry
        qb, kb, vb, ib, ab = xs
        d = jnp.where(causal, ab[..., :, None] - ab[..., None, :] + ib[..., None, :], -jnp.inf)
        inter = ab + m[..., None]
        m_t = jnp.maximum(inter, jnp.max(d, axis=-1))
        w_intra = jnp.exp(d - m_t[..., None])
        w_inter = jnp.exp(inter - m_t)
        s = jnp.einsum('bhtd,bhsd->bhts', qb, kb) * w_intra
        num = jnp.einsum('bhts,bhsv->bhtv', s, vb) + w_inter[..., None] * jnp.einsum('bhtd,bhvd->bhtv', qb, c_mat)
        den = jnp.sum(s, axis=-1) + w_inter * jnp.einsum('bhtd,bhd->bht', qb, n_vec)
        h = num / jnp.maximum(jnp.abs(den), jnp.exp(-m_t))[..., None]
        a_last = ab[..., -1]
        m_new = m_t[..., -1]
        w_s = jnp.exp(a_last[..., None] - ab + ib - m_new[..., None])
        decay = jnp.exp(a_last + m - m_new)
        c_new = decay[..., None, None] * c_mat + jnp.einsum('bhs,bhsv,bhsd->bhvd', w_s, vb, kb)
        n_new = decay[..., None] * n_vec + jnp.einsum('bhs,bhsd->bhd', w_s, kb)
        return (c_new, n_new, m_new), h

    init = (jnp.zeros((bsz, heads, dv, dk), f32), jnp.zeros((bsz, heads, dk), f32), jnp.zeros((bsz, heads), f32))
    _, hs = lax.scan(step, init, (qc, kc, vc, ic, ac))
    return hs.transpose(1, 0, 3, 2, 4).reshape(bsz, seq, heads, dv).astype(out_dtype)


def even_mixer(hn, w_in, w_conv, g_v, w_s, b_s, w_out):
    z = hn @ w_in
    b_gate, c_gate, x_in, u, v = jnp.split(z, [CONV_DIM, 2 * CONV_DIM, 3 * CONV_DIM, 3 * CONV_DIM + GMLP_DIM], axis=-1)
    y_a = short_gated_conv(b_gate, c_gate, x_in, w_conv)
    y_b = chunked_spatial_gating(u, v, g_v, w_s, b_s)
    return jnp.concatenate([y_a, y_b], axis=-1) @ w_out


def odd_mixer(hn, positions, w_in, b_gate, g_qa, g_kva, w_q_up, w_kv_up, g_q, g_k, g_mh, w_out):
    bsz, seq, _ = hn.shape
    z = hn @ w_in
    sizes = [Q_LORA, KV_LORA, MLA_ROPE, MLSTM_HEADS * MLSTM_QK, MLSTM_HEADS * MLSTM_QK,
             MLSTM_HEADS * MLSTM_V, MLSTM_HEADS * MLSTM_V, MLSTM_HEADS, MLSTM_HEADS]
    q_lat, kv_lat, k_pe, mq, mk, mv, mo, mi, mf = jnp.split(z, np.cumsum(sizes)[:-1].tolist(), axis=-1)
    y_c = latent_attention(q_lat, kv_lat, k_pe, positions, g_qa, g_kva, w_q_up, w_kv_up, g_q, g_k)
    h_m = mlstm_chunkwise(mq.reshape(bsz, seq, MLSTM_HEADS, MLSTM_QK),
                          mk.reshape(bsz, seq, MLSTM_HEADS, MLSTM_QK),
                          mv.reshape(bsz, seq, MLSTM_HEADS, MLSTM_V),
                          mi + b_gate[:MLSTM_HEADS], mf + b_gate[MLSTM_HEADS:])
    y_d = jax.nn.sigmoid(mo) * rmsnorm(h_m, g_mh).reshape(bsz, seq, MLSTM_HEADS * MLSTM_V)
    return jnp.concatenate([y_c, y_d], axis=-1) @ w_out


def setup_inputs(seed: int = 0) -> dict:
    key = jax.random.key(seed)
    ks = list(jax.random.split(key, 40))
    f32 = jnp.float32

    def normal(shape):
        return jax.random.normal(ks.pop(), shape, f32)

    def dense(shape, fan_in):
        return normal(shape) * fan_in ** -0.5

    def gain(shape):
        return 1.0 + 0.05 * normal(shape)

    x = normal((BATCH, SEQ, D_MODEL))
    p = normal((DEPTH, BATCH, SEQ, D_PLE))
    positions = jnp.arange(SEQ, dtype=jnp.int32)[None, :] + jax.random.randint(ks.pop(), (BATCH, 1), 0, SEQ, dtype=jnp.int32)
    od_b_gate = jnp.concatenate([0.1 * normal((N_ODD, MLSTM_HEADS)), 3.0 + 0.1 * normal((N_ODD, MLSTM_HEADS))], axis=-1)
    return {
        'x': x,
        'p': p,
        'positions': positions,
        'g_mix': gain((DEPTH, D_MODEL)),
        'g_ffn': gain((DEPTH, D_MODEL)),
        'g_ple': gain((DEPTH, D_MODEL)),
        'ev_w_in': dense((N_EVEN, D_MODEL, EVEN_IN), D_MODEL),
        'ev_w_conv': dense((N_EVEN, CONV_WIDTH, CONV_DIM), CONV_WIDTH),
        'ev_g_v': gain((N_EVEN, GMLP_HEADS, GMLP_HEAD_DIM)),
        'ev_w_s': dense((N_EVEN, GMLP_HEADS, GMLP_CHUNK, GMLP_CHUNK), GMLP_CHUNK),
        'ev_b_s': 1.0 + 0.1 * normal((N_EVEN, GMLP_HEADS, GMLP_CHUNK)),
        'ev_w_out': dense((N_EVEN, EVEN_MIX, D_MODEL), EVEN_MIX),
        'od_w_in': dense((N_ODD, D_MODEL, ODD_IN), D_MODEL),
        'od_b_gate': od_b_gate,
        'od_g_qa': gain((N_ODD, Q_LORA)),
        'od_g_kva': gain((N_ODD, KV_LORA)),
        'od_w_q_up': dense((N_ODD, Q_LORA, MLA_HEADS * MLA_QK), Q_LORA),
        'od_w_kv_up': dense((N_ODD, KV_LORA, MLA_HEADS * (MLA_NOPE + MLA_V)), KV_LORA),
        'od_g_q': gain((N_ODD, MLA_QK)),
        'od_g_k': gain((N_ODD, MLA_QK)),
        'od_g_mh': gain((N_ODD, MLSTM_HEADS, MLSTM_V)),
        'od_w_out': dense((N_ODD, ODD_MIX, D_MODEL), ODD_MIX),
        'w_gate': dense((DEPTH, D_MODEL, D_FF), D_MODEL),
        'w_up': dense((DEPTH, D_MODEL, D_FF), D_MODEL),
        'w_down': dense((DEPTH, D_FF, D_MODEL), D_FF),
        'w_ple_proj': dense((DEPTH, D_PLE, D_MODEL), D_PLE),
        'w_ple_gate': dense((DEPTH, D_MODEL, D_MODEL), D_MODEL),
    }


def reference(x, p, positions, g_mix, g_ffn, g_ple, ev_w_in, ev_w_conv, ev_g_v, ev_w_s, ev_b_s, ev_w_out,
              od_w_in, od_b_gate, od_g_qa, od_g_kva, od_w_q_up, od_w_kv_up, od_g_q, od_g_k, od_g_mh, od_w_out,
              w_gate, w_up, w_down, w_ple_proj, w_ple_gate):
    h = x
    for layer in range(DEPTH):
        j = layer // 2
        hn = rmsnorm(h, g_mix[layer])
        if layer % 2 == 0:
            mix = even_mixer(hn, ev_w_in[j], ev_w_conv[j], ev_g_v[j], ev_w_s[j], ev_b_s[j], ev_w_out[j])
        else:
            mix = odd_mixer(hn, positions, od_w_in[j], od_b_gate[j], od_g_qa[j], od_g_kva[j], od_w_q_up[j],
                            od_w_kv_up[j], od_g_q[j], od_g_k[j], od_g_mh[j], od_w_out[j])
        h = h + mix
        h = h + swiglu(rmsnorm(h, g_ffn[layer]), w_gate[layer], w_up[layer], w_down[layer])
        gate = jax.nn.sigmoid(rmsnorm(h, g_ple[layer]) @ w_ple_gate[layer])
        h = h + gate * (p[layer] @ w_ple_proj[layer])
    return h
```

```python
import functools
import math

import numpy as np
import jax
import jax.numpy as jnp
from jax import lax
from jax.experimental import pallas as pl
from jax.experimental.pallas import tpu as pltpu

F32 = jnp.float32
BF16 = jnp.bfloat16

D_MODEL = 1024
D_PLE = 256
EPS = 1e-6

CONV_DIM = 512
GMLP_DIM = 512
GMLP_HEADS = 8
GMLP_HEAD_DIM = 64
GMLP_CHUNK = 128
EVEN_IN = 3 * CONV_DIM + 2 * GMLP_DIM

MLA_HEADS = 8
MLA_NOPE = 64
MLA_ROPE = 32
MLA_V = 64
MLA_QK = MLA_NOPE + MLA_ROPE
Q_LORA = 384
KV_LORA = 256
ROPE_THETA = 10000.0
MLSTM_HEADS = 4
MLSTM_QK = 64
MLSTM_V = 128
D_FF = 2816

LANES = 128
MXU_EDGE = 256

TM = 512
TQ = 512
TK = 512
ML = 256
HEAD_PAD = 128
ODD_IN_PAD = 2304
TAIL_OFF = ODD_IN_PAD - LANES
FF_CHUNKS = ((0, 1024), (1024, 2048), (2048, 2816))

VMEM_LIMIT = 56 * 1024 * 1024


def _dot(a, b):
    return jnp.dot(a, b, preferred_element_type=F32)


def _dot_nt(a, b):
    return lax.dot_general(a, b, (((1,), (1,)), ((), ())), preferred_element_type=F32)


def _rms_rows(x, g):
    ms = jnp.mean(x * x, axis=-1, keepdims=True)
    return x * lax.rsqrt(ms + EPS) * g


def _rms_cols(x, g):
    ms = jnp.mean(x * x, axis=0, keepdims=True)
    return x * lax.rsqrt(ms + EPS) * g


def _const_spec(shape):
    zeros = (0,) * len(shape)
    return pl.BlockSpec(shape, lambda *_: zeros, pipeline_mode=pl.Buffered(1))


def _params(n_axes):
    return pltpu.CompilerParams(dimension_semantics=("arbitrary",) * n_axes,
                                vmem_limit_bytes=VMEM_LIMIT)


def _even_kernel(h_ref, gmix_ref, win_ref, wconv_ref, gmat_ref, gv_ref, wpair_ref, bm_ref,
                 ya_ref, yb_ref, zbuf_ref):
    @pl.when(pl.program_id(1) == 0)
    def _():
        zbuf_ref[0:8, :] = jnp.zeros((8, CONV_DIM), F32)

    x = h_ref[0]
    hn = _rms_rows(x, gmix_ref[...]).astype(BF16)
    z = _dot(hn, win_ref[...])
    b_gate = z[:, 0:512]
    c_gate = z[:, 512:1024]
    x_in = z[:, 1024:1536]
    u = z[:, 1536:2048]
    v = z[:, 2048:2560]

    zz = c_gate * x_in
    zbuf_ref[8:8 + TM, :] = zz
    z1 = zbuf_ref[7:7 + TM, :]
    z2 = zbuf_ref[6:6 + TM, :]
    wc = wconv_ref[...]
    conv = wc[2:3] * zz + wc[1:2] * z1 + wc[0:1] * z2
    ya_ref[0] = (b_gate * conv).astype(BF16)
    zbuf_ref[0:8, :] = zz[TM - 8:TM, :]

    gu = jax.nn.gelu(u)
    gv = jax.nn.gelu(v)
    ss = _dot((gv * gv).astype(BF16), gmat_ref[...])
    vn = gv * lax.rsqrt(ss * (1.0 / GMLP_HEAD_DIM) + EPS) * gv_ref[...]
    lane = lax.broadcasted_iota(jnp.int32, (GMLP_CHUNK, LANES), 1)
    row_t = lax.broadcasted_iota(jnp.int32, (GMLP_CHUNK, 2 * GMLP_CHUNK), 0)
    col_s = lax.broadcasted_iota(jnp.int32, (GMLP_CHUNK, 2 * GMLP_CHUNK), 1) % GMLP_CHUNK
    tril = col_s <= row_t
    wms = [jnp.where(tril, wpair_ref[j], 0.0).astype(BF16) for j in range(GMLP_HEADS // 2)]
    for c in range(TM // GMLP_CHUNK):
        r0 = c * GMLP_CHUNK
        outs = []
        for j in range(GMLP_HEADS // 2):
            vp = vn[r0:r0 + GMLP_CHUNK, j * LANES:(j + 1) * LANES]
            lo = jnp.where(lane < GMLP_HEAD_DIM, vp, 0.0).astype(BF16)
            hi = jnp.where(lane >= GMLP_HEAD_DIM, vp, 0.0).astype(BF16)
            outs.append(_dot(wms[j], jnp.concatenate([lo, hi], axis=0)))
        mixed = jnp.concatenate(outs, axis=1) + bm_ref[...]
        yb_ref[0, r0:r0 + GMLP_CHUNK, :] = (gu[r0:r0 + GMLP_CHUNK] * mixed).astype(BF16)


def _even_mixer(h, g_mix, w_in, w_conv, g_v, w_s, b_s):
    bsz, seq, _ = h.shape
    gmat = jnp.asarray(np.kron(np.eye(GMLP_HEADS), np.ones((GMLP_HEAD_DIM, GMLP_HEAD_DIM))), BF16)
    wpair = w_s.reshape(GMLP_HEADS // 2, 2, GMLP_CHUNK, GMLP_CHUNK).transpose(0, 2, 1, 3)
    wpair = wpair.reshape(GMLP_HEADS // 2, GMLP_CHUNK, 2 * GMLP_CHUNK)
    bm = jnp.repeat(b_s.T, GMLP_HEAD_DIM, axis=1)
    out_sds = jax.ShapeDtypeStruct((bsz, seq, CONV_DIM), BF16)
    tile = lambda w: pl.BlockSpec((1, TM, w), lambda b, i: (b, i, 0))
    return pl.pallas_call(
        _even_kernel,
        grid=(bsz, seq // TM),
        in_specs=[tile(D_MODEL), _const_spec((1, D_MODEL)), _const_spec((D_MODEL, EVEN_IN)),
                  _const_spec((3, CONV_DIM)), _const_spec((GMLP_DIM, GMLP_DIM)), _const_spec((1, GMLP_DIM)),
                  _const_spec((GMLP_HEADS // 2, GMLP_CHUNK, 2 * GMLP_CHUNK)),
                  _const_spec((GMLP_CHUNK, GMLP_DIM))],
        out_specs=[tile(CONV_DIM), tile(GMLP_DIM)],
        out_shape=[out_sds, out_sds],
        scratch_shapes=[pltpu.VMEM((TM + 8, CONV_DIM), F32)],
        compiler_params=_params(2),
        name="even_mixer",
    )(h, g_mix.reshape(1, D_MODEL), w_in.astype(BF16), w_conv, gmat, g_v.reshape(1, GMLP_DIM), wpair, bm)


def _ffn_kernel(h_ref, y1_ref, y2_ref, p_ref, wout_ref, gffn_ref, wg_ref, wu_ref, wd_ref,
                gple_ref, wpg_ref, wpp_ref, o_ref):
    y = jnp.concatenate([y1_ref[0], y2_ref[0]], axis=-1)
    x = h_ref[0] + _dot(y, wout_ref[...])
    hn = _rms_rows(x, gffn_ref[...]).astype(BF16)
    acc = x
    for lo, hi in FF_CHUNKS:
        g = _dot(hn, wg_ref[:, lo:hi])
        u = _dot(hn, wu_ref[:, lo:hi])
        a = (g * jax.nn.sigmoid(g) * u).astype(BF16)
        acc = acc + _dot(a, wd_ref[lo:hi, :])
    hn2 = _rms_rows(acc, gple_ref[...]).astype(BF16)
    gate = jax.nn.sigmoid(_dot(hn2, wpg_ref[...]))
    pp = _dot(p_ref[0].astype(BF16), wpp_ref[...])
    o_ref[0] = acc + gate * pp


def _ffn_ple(h, y1, y2, p, w_out, g_ffn, w_gate, w_up, w_down, g_ple, w_pg, w_pp):
    bsz, seq, _ = h.shape
    tile = lambda w: pl.BlockSpec((1, TM, w), lambda b, i: (b, i, 0))
    return pl.pallas_call(
        _ffn_kernel,
        grid=(bsz, seq // TM),
        in_specs=[tile(D_MODEL), tile(512), tile(512), tile(D_PLE),
                  _const_spec((D_MODEL, D_MODEL)), _const_spec((1, D_MODEL)),
                  _const_spec((D_MODEL, D_FF)), _const_spec((D_MODEL, D_FF)), _const_spec((D_FF, D_MODEL)),
                  _const_spec((1, D_MODEL)), _const_spec((D_MODEL, D_MODEL)), _const_spec((D_PLE, D_MODEL))],
        out_specs=tile(D_MODEL),
        out_shape=jax.ShapeDtypeStruct((bsz, seq, D_MODEL), F32),
        compiler_params=_params(2),
        name="ffn_ple",
    )(h, y1, y2, p, w_out.astype(BF16), g_ffn.reshape(1, D_MODEL), w_gate.astype(BF16), w_up.astype(BF16),
      w_down.astype(BF16), g_ple.reshape(1, D_MODEL), w_pg.astype(BF16), w_pp.astype(BF16))


def _log_sigmoid(x):
    return jnp.minimum(x, 0.0) - jnp.log1p(jnp.exp(-jnp.abs(x)))


def _rope_cols(xr, cos, sin):
    half = MLA_ROPE // 2
    x1, x2 = xr[:half], xr[half:]
    return x1 * cos - x2 * sin, x2 * cos + x1 * sin


def _odd_kernel(h_ref, pos_ref, gmix_ref, win_ref, bias_ref, gqa_ref, gkva_ref, wqt_ref, wkvt_ref,
                gqn_ref, gqr_ref, gkn_ref, gkr_ref, freq_ref,
                qt_ref, k_ref, vt_ref, mq_ref, mkt_ref, mv_ref, mo_ref, g_ref, gt_ref):
    x = h_ref[0]
    hn = _rms_rows(x, gmix_ref[...]).astype(BF16)
    z = _dot(hn, win_ref[...])
    q_lat = z[:, 0:384]
    kv_lat = z[:, 384:640]

    mq_ref[0] = (z[:, 640:896] * (MLSTM_QK ** -0.5)).astype(BF16)
    mkt_ref[0] = z[:, 896:1152].T.astype(BF16)
    mv_ref[0] = z[:, 1152:1664].astype(BF16)
    mo_ref[0] = z[:, 1664:2176]
    tail = z[:, TAIL_OFF:ODD_IN_PAD] + bias_ref[...]
    lane = lax.broadcasted_iota(jnp.int32, (TM, LANES), 1)
    f_lane = (lane >= MLA_ROPE + MLSTM_HEADS) & (lane < MLA_ROPE + 2 * MLSTM_HEADS)
    tail = jnp.where(f_lane, _log_sigmoid(tail), tail)
    g_ref[0] = tail
    tail_t = tail.T
    gt_ref[0] = tail_t[MLA_ROPE:MLA_ROPE + 2 * MLSTM_HEADS]

    qn = _rms_rows(q_lat, gqa_ref[...]).astype(BF16)
    kvn = _rms_rows(kv_lat, gkva_ref[...]).astype(BF16)
    q_t = _dot_nt(wqt_ref[...], qn)
    kv_t = _dot_nt(wkvt_ref[...], kvn)

    ang = freq_ref[...] * pos_ref[0].astype(F32)
    cos = jnp.cos(ang)
    sin = jnp.sin(ang)
    kr1, kr2 = _rope_cols(_rms_cols(tail_t[0:MLA_ROPE], gkr_ref[...]), cos, sin)
    pad = jnp.zeros((HEAD_PAD - MLA_QK, TM), F32)
    scale = MLA_QK ** -0.5
    for hd in range(MLA_HEADS):
        q0 = hd * MLA_QK
        q_nope = _rms_cols(q_t[q0:q0 + MLA_NOPE], gqn_ref[...])
        qr1, qr2 = _rope_cols(_rms_cols(q_t[q0 + MLA_NOPE:q0 + MLA_QK], gqr_ref[...]), cos, sin)
        q_full = jnp.concatenate([q_nope, qr1, qr2, pad], axis=0) * scale
        qt_ref[0, hd] = q_full.astype(BF16)
        k0 = hd * (MLA_NOPE + MLA_V)
        k_nope = _rms_cols(kv_t[k0:k0 + MLA_NOPE], gkn_ref[...])
        k_full = jnp.concatenate([k_nope, kr1, kr2, pad], axis=0)
        k_ref[0, hd] = k_full.T.astype(BF16)
        vt_ref[0, 0, hd] = kv_t[k0 + MLA_NOPE:k0 + MLA_NOPE + MLA_V].astype(BF16)


def _odd_proj(h, positions, g_mix, w_in, b_gate, g_qa, g_kva, w_q_up, w_kv_up, g_q, g_k):
    bsz, seq, _ = h.shape
    c = np.cumsum([0, Q_LORA, KV_LORA, MLA_ROPE, 256, 256, 512, 512, MLSTM_HEADS, MLSTM_HEADS])
    seg = lambda i: w_in[:, c[i]:c[i + 1]]
    tail_pad = jnp.zeros((D_MODEL, LANES - MLA_ROPE - 2 * MLSTM_HEADS), w_in.dtype)
    w_in_r = jnp.concatenate([seg(0), seg(1), seg(3), seg(4), seg(5), seg(6), seg(2), seg(7), seg(8), tail_pad],
                             axis=1).astype(BF16)
    bias = jnp.zeros((1, LANES), F32).at[0, MLA_ROPE:MLA_ROPE + 2 * MLSTM_HEADS].set(b_gate)
    inv_freq = ROPE_THETA ** (-jnp.arange(0, MLA_ROPE, 2, dtype=F32) / MLA_ROPE)
    col = lambda a: a.reshape(-1, 1)
    tile3 = lambda w: pl.BlockSpec((1, TM, w), lambda b, i: (b, i, 0))
    out_shape = [
        jax.ShapeDtypeStruct((bsz, MLA_HEADS, HEAD_PAD, seq), BF16),
        jax.ShapeDtypeStruct((bsz, MLA_HEADS, seq, HEAD_PAD), BF16),
        jax.ShapeDtypeStruct((bsz, seq // TK, MLA_HEADS, MLA_V, TK), BF16),
        jax.ShapeDtypeStruct((bsz, seq, 256), BF16),
        jax.ShapeDtypeStruct((bsz, 256, seq), BF16),
        jax.ShapeDtypeStruct((bsz, seq, 512), BF16),
        jax.ShapeDtypeStruct((bsz, seq, 512), F32),
        jax.ShapeDtypeStruct((bsz, seq, LANES), F32),
        jax.ShapeDtypeStruct((bsz, 2 * MLSTM_HEADS, seq), F32),
    ]
    out_specs = [
        pl.BlockSpec((1, MLA_HEADS, HEAD_PAD, TM), lambda b, i: (b, 0, 0, i)),
        pl.BlockSpec((1, MLA_HEADS, TM, HEAD_PAD), lambda b, i: (b, 0, i, 0)),
        pl.BlockSpec((1, 1, MLA_HEADS, MLA_V, TK), lambda b, i: (b, i, 0, 0, 0)),
        tile3(256),
        pl.BlockSpec((1, 256, TM), lambda b, i: (b, 0, i)),
        tile3(512), tile3(512), tile3(LANES),
        pl.BlockSpec((1, 2 * MLSTM_HEADS, TM), lambda b, i: (b, 0, i)),
    ]
    return pl.pallas_call(
        _odd_kernel,
        grid=(bsz, seq // TM),
        in_specs=[tile3(D_MODEL), pl.BlockSpec((1, 1, TM), lambda b, i: (b, 0, i)),
                  _const_spec((1, D_MODEL)), _const_spec((D_MODEL, ODD_IN_PAD)), _const_spec((1, LANES)),
                  _const_spec((1, Q_LORA)), _const_spec((1, KV_LORA)),
                  _const_spec((MLA_HEADS * MLA_QK, Q_LORA)), _const_spec((MLA_HEADS * (MLA_NOPE + MLA_V), KV_LORA)),
                  _const_spec((MLA_NOPE, 1)), _const_spec((MLA_ROPE, 1)),
                  _const_spec((MLA_NOPE, 1)), _const_spec((MLA_ROPE, 1)), _const_spec((MLA_ROPE // 2, 1))],
        out_specs=out_specs,
        out_shape=out_shape,
        compiler_params=_params(2),
        name="odd_proj",
    )(h, positions.reshape(bsz, 1, seq), g_mix.reshape(1, D_MODEL), w_in_r, bias,
      g_qa.reshape(1, Q_LORA), g_kva.reshape(1, KV_LORA), w_q_up.T.astype(BF16), w_kv_up.T.astype(BF16),
      col(g_q[:MLA_NOPE]), col(g_q[MLA_NOPE:]), col(g_k[:MLA_NOPE]), col(g_k[MLA_NOPE:]),
      inv_freq.reshape(-1, 1))


def _attn_kernel(qt_ref, k_ref, vt_ref, o_ref, ot_ref):
    qi = pl.program_id(1)
    row_s = lax.broadcasted_iota(jnp.int32, (TK, TQ), 0)
    col_t = lax.broadcasted_iota(jnp.int32, (TK, TQ), 1)
    visible = row_s <= col_t

    def one_head(hd, _):
        q_t = qt_ref[0, hd]

        def tile(j, carry, masked):
            m, l, acc = carry
            k = k_ref[0, hd, pl.ds(pl.multiple_of(j * TK, TK), TK), :]
            s = _dot(k, q_t)
            if masked:
                s = jnp.where(visible, s, -jnp.inf)
            m_new = jnp.maximum(m, jnp.max(s, axis=0, keepdims=True))
            alpha = jnp.exp(m - m_new)
            p = jnp.exp(s - m_new)
            l = alpha * l + jnp.sum(p, axis=0, keepdims=True)
            acc = alpha * acc + _dot(vt_ref[0, j, hd], p.astype(BF16))
            return m_new, l, acc

        init = (jnp.full((1, TQ), -jnp.inf, F32), jnp.zeros((1, TQ), F32), jnp.zeros((MLA_V, TQ), F32))
        carry = lax.fori_loop(0, qi, lambda j, c: tile(j, c, False), init)
        _, l, acc = tile(qi, carry, True)
        ot_ref[hd] = acc / l
        return 0

    lax.fori_loop(0, MLA_HEADS, one_head, 0)
    o_ref[0] = ot_ref[...].reshape(MLA_HEADS * MLA_V, TQ).T.astype(BF16)


def _attention(qt, k, vt):
    bsz, _, _, seq = qt.shape
    return pl.pallas_call(
        _attn_kernel,
        grid=(bsz, seq // TQ),
        in_specs=[pl.BlockSpec((1, MLA_HEADS, HEAD_PAD, TQ), lambda b, i: (b, 0, 0, i)),
                  pl.BlockSpec((1, MLA_HEADS, seq, HEAD_PAD), lambda b, i: (b, 0, 0, 0),
                               pipeline_mode=pl.Buffered(1)),
                  pl.BlockSpec((1, seq // TK, MLA_HEADS, MLA_V, TK), lambda b, i: (b, 0, 0, 0, 0),
                               pipeline_mode=pl.Buffered(1))],
        out_specs=pl.BlockSpec((1, TQ, MLA_HEADS * MLA_V), lambda b, i: (b, i, 0)),
        out_shape=jax.ShapeDtypeStruct((bsz, seq, MLA_HEADS * MLA_V), BF16),
        scratch_shapes=[pltpu.VMEM((MLA_HEADS, MLA_V, TQ), F32)],
        compiler_params=_params(2),
        name="attention",
    )(qt, k, vt)


def _split3(x):
    hi = x.astype(BF16)
    r = x - hi.astype(F32)
    mid = r.astype(BF16)
    lo = (r - mid.astype(F32)).astype(BF16)
    return hi, mid, lo


def _mlstm_kernel(mq_ref, mkt_ref, mv_ref, mo_ref, g_ref, gt_ref, gmh_ref, yd_ref, ct_ref, m_ref):
    bsz = mq_ref.shape[0]

    @pl.when(pl.program_id(0) == 0)
    def _():
        ct_ref[...] = jnp.zeros(ct_ref.shape, F32)
        m_ref[...] = jnp.zeros(m_ref.shape, F32)

    row_t = lax.broadcasted_iota(jnp.int32, (ML, ML), 0)
    col_s = lax.broadcasted_iota(jnp.int32, (ML, ML), 1)
    causal = col_s <= row_t
    lower = jnp.where(causal, 1.0, 0.0).astype(BF16)
    upper = jnp.where(row_t <= col_s, 1.0, 0.0).astype(BF16)
    lane = lax.broadcasted_iota(jnp.int32, (ML, LANES), 1)
    ones_col = jnp.where(lane == 0, 1.0, 0.0).astype(BF16)
    q_lo = lane < MLSTM_QK

    for b in range(bsz):
        gcol = g_ref[b]
        grow = gt_ref[b]
        a_cols = sum(_dot(lower, piece) for piece in _split3(gcol))
        a_rows = sum(_dot(piece, upper) for piece in _split3(jnp.concatenate([grow, grow], axis=0)))
        for hd in range(MLSTM_HEADS):
            pair, half = hd // 2, hd % 2
            idx = b * MLSTM_HEADS + hd
            a_col = a_cols[:, MLA_ROPE + MLSTM_HEADS + hd:MLA_ROPE + MLSTM_HEADS + hd + 1]
            a_row = a_rows[MLSTM_HEADS + hd:MLSTM_HEADS + hd + 1]
            i_row = grow[hd:hd + 1]
            m_prev = m_ref[idx][0:1, 0:1]

            d = jnp.where(causal, a_col - a_row + i_row, -jnp.inf)
            inter = a_col + m_prev
            m_t = jnp.maximum(inter, jnp.max(d, axis=1, keepdims=True))
            w_intra = jnp.exp(d - m_t)
            w_inter = jnp.exp(inter - m_t)

            q_pair = mq_ref[b, :, pair * LANES:(pair + 1) * LANES]
            q_m = jnp.where(q_lo if half == 0 else ~q_lo, q_pair.astype(F32), 0.0).astype(BF16)
            kt_pair = mkt_ref[b, pair * LANES:(pair + 1) * LANES, :]
            s = _dot(q_m, kt_pair) * w_intra
            v_ext = jnp.concatenate([mv_ref[b, :, hd * LANES:(hd + 1) * LANES], ones_col], axis=1)
            nv = _dot(s.astype(BF16), v_ext)
            qc = _dot(q_m, ct_ref[b, pair].astype(BF16))
            num = nv[:, :MLSTM_V] + w_inter * qc[:, :MLSTM_V]
            den = jnp.sum(s, axis=1, keepdims=True) + w_inter * qc[:, MLSTM_V:MLSTM_V + 1]
            hh = num / jnp.maximum(jnp.abs(den), jnp.exp(-m_t))

            a_last = a_row[:, ML - 1:ML]
            m_new = m_t[ML - 1:ML, :]
            w_s = jnp.exp(a_last - a_row + i_row - m_new)
            decay = jnp.exp(a_last + m_prev - m_new)
            kt = mkt_ref[b, hd * MLSTM_QK:(hd + 1) * MLSTM_QK, :].astype(F32)
            upd = _dot((kt * w_s).astype(BF16), v_ext)
            r0 = half * MLSTM_QK
            ct_ref[b, pair, r0:r0 + MLSTM_QK, :] = decay * ct_ref[b, pair, r0:r0 + MLSTM_QK, :] + upd
            m_ref[idx] = jnp.broadcast_to(m_new, (8, LANES))

            hn = _rms_rows(hh, gmh_ref[:, hd * LANES:(hd + 1) * LANES])
            gate = jax.nn.sigmoid(mo_ref[b, :, hd * LANES:(hd + 1) * LANES])
            yd_ref[b, :, hd * LANES:(hd + 1) * LANES] = (gate * hn).astype(BF16)


def _mlstm(mq, mkt, mv, mo, g, gt, g_mh):
    bsz, seq, _ = mq.shape
    tile3 = lambda w: pl.BlockSpec((bsz, ML, w), lambda i: (0, i, 0))
    return pl.pallas_call(
        _mlstm_kernel,
        grid=(seq // ML,),
        in_specs=[tile3(256), pl.BlockSpec((bsz, 256, ML), lambda i: (0, 0, i)), tile3(512), tile3(512),
                  tile3(LANES), pl.BlockSpec((bsz, 2 * MLSTM_HEADS, ML), lambda i: (0, 0, i)),
                  _const_spec((1, MLSTM_HEADS * MLSTM_V))],
        out_specs=tile3(512),
        out_shape=jax.ShapeDtypeStruct((bsz, seq, MLSTM_HEADS * MLSTM_V), BF16),
        scratch_shapes=[pltpu.VMEM((bsz, MLSTM_HEADS // 2, 2 * MLSTM_QK, 2 * MLSTM_V), F32),
                        pltpu.VMEM((bsz * MLSTM_HEADS, 8, LANES), F32)],
        compiler_params=_params(1),
        name="mlstm",
    )(mq, mkt, mv, mo, g, gt, g_mh.reshape(1, MLSTM_HEADS * MLSTM_V))


def kernel(x, p, positions, g_mix, g_ffn, g_ple, ev_w_in, ev_w_conv, ev_g_v, ev_w_s, ev_b_s, ev_w_out,
           od_w_in, od_b_gate, od_g_qa, od_g_kva, od_w_q_up, od_w_kv_up, od_g_q, od_g_k, od_g_mh, od_w_out,
           w_gate, w_up, w_down, w_ple_proj, w_ple_gate):
    h = x
    depth = g_mix.shape[0]
    for layer in range(depth):
        j = layer // 2
        if layer % 2 == 0:
            y1, y2 = _even_mixer(h, g_mix[layer], ev_w_in[j], ev_w_conv[j], ev_g_v[j], ev_w_s[j], ev_b_s[j])
            w_out = ev_w_out[j]
        else:
            qt, k, vt, mq, mkt, mv, mo, g, gt = _odd_proj(
                h, positions, g_mix[layer], od_w_in[j], od_b_gate[j], od_g_qa[j], od_g_kva[j],
                od_w_q_up[j], od_w_kv_up[j], od_g_q[j], od_g_k[j])
            y1 = _attention(qt, k, vt)
            y2 = _mlstm(mq, mkt, mv, mo, g, gt, od_g_mh[j])
            w_out = od_w_out[j]
        h = _ffn_ple(h, y1, y2, p[layer], w_out, g_ffn[layer], w_gate[layer], w_up[layer], w_down[layer],
                     g_ple[layer], w_ple_gate[layer], w_ple_proj[layer])
    return h
```

```python
import functools
import math

import numpy as np
import jax
import jax.numpy as jnp
from jax import lax
from jax.experimental import pallas as pl
from jax.experimental.pallas import tpu as pltpu

F32 = jnp.float32
BF16 = jnp.bfloat16

D_MODEL = 1024
D_PLE = 256
EPS = 1e-6

CONV_DIM = 512
GMLP_DIM = 512
GMLP_HEADS = 8
GMLP_HEAD_DIM = 64
GMLP_CHUNK = 128
EVEN_IN = 3 * CONV_DIM + 2 * GMLP_DIM

MLA_HEADS = 8
MLA_NOPE = 64
MLA_ROPE = 32
MLA_V = 64
MLA_QK = MLA_NOPE + MLA_ROPE
Q_LORA = 384
KV_LORA = 256
ROPE_THETA = 10000.0
MLSTM_HEADS = 4
MLSTM_QK = 64
MLSTM_V = 128
D_FF = 2816

LANES = 128
MXU_EDGE = 256

TM = 512
TQ = 512
TK = 512
ML = 256
HEAD_PAD = 128
ODD_IN_PAD = 2304
TAIL_OFF = ODD_IN_PAD - LANES
FF_CHUNKS = ((0, 1024), (1024, 2048), (2048, 2816))

VMEM_LIMIT = 56 * 1024 * 1024


def _dot(a, b):
    return jnp.dot(a, b, preferred_element_type=F32)


def _dot_nt(a, b):
    return lax.dot_general(a, b, (((1,), (1,)), ((), ())), preferred_element_type=F32)


def _rms_rows(x, g):
    ms = jnp.mean(x * x, axis=-1, keepdims=True)
    return x * lax.rsqrt(ms + EPS) * g


def _rms_cols(x, g):
    ms = jnp.mean(x * x, axis=0, keepdims=True)
    return x * lax.rsqrt(ms + EPS) * g


def _const_spec(shape):
    zeros = (0,) * len(shape)
    return pl.BlockSpec(shape, lambda *_: zeros, pipeline_mode=pl.Buffered(1))


def _params(n_axes):
    return pltpu.CompilerParams(dimension_semantics=("arbitrary",) * n_axes,
                                vmem_limit_bytes=VMEM_LIMIT)


def _even_kernel(h_ref, gmix_ref, win_ref, wconv_ref, gmat_ref, gv_ref, wpair_ref, bm_ref,
                 ya_ref, yb_ref, zbuf_ref):
    @pl.when(pl.program_id(1) == 0)
    def _():
        zbuf_ref[0:8, :] = jnp.zeros((8, CONV_DIM), F32)

    x = h_ref[0]
    hn = _rms_rows(x, gmix_ref[...]).astype(BF16)
    z = _dot(hn, win_ref[...])
    b_gate = z[:, 0:512]
    c_gate = z[:, 512:1024]
    x_in = z[:, 1024:1536]
    u = z[:, 1536:2048]
    v = z[:, 2048:2560]

    zz = c_gate * x_in
    zbuf_ref[8:8 + TM, :] = zz
    z1 = zbuf_ref[7:7 + TM, :]
    z2 = zbuf_ref[6:6 + TM, :]
    wc = wconv_ref[...]
    conv = wc[2:3] * zz + wc[1:2] * z1 + wc[0:1] * z2
    ya_ref[0] = (b_gate * conv).astype(BF16)
    zbuf_ref[0:8, :] = zz[TM - 8:TM, :]

    gu = jax.nn.gelu(u)
    gv = jax.nn.gelu(v)
    ss = _dot((gv * gv).astype(BF16), gmat_ref[...])
    vn = gv * lax.rsqrt(ss * (1.0 / GMLP_HEAD_DIM) + EPS) * gv_ref[...]
    lane = lax.broadcasted_iota(jnp.int32, (GMLP_CHUNK, LANES), 1)
    row_t = lax.broadcasted_iota(jnp.int32, (GMLP_CHUNK, 2 * GMLP_CHUNK), 0)
    col_s = lax.broadcasted_iota(jnp.int32, (GMLP_CHUNK, 2 * GMLP_CHUNK), 1) % GMLP_CHUNK
    tril = col_s <= row_t
    wms = [jnp.where(tril, wpair_ref[j], 0.0).astype(BF16) for j in range(GMLP_HEADS // 2)]
    for c in range(TM // GMLP_CHUNK):
        r0 = c * GMLP_CHUNK
        outs = []
        for j in range(GMLP_HEADS // 2):
            vp = vn[r0:r0 + GMLP_CHUNK, j * LANES:(j + 1) * LANES]
            lo = jnp.where(lane < GMLP_HEAD_DIM, vp, 0.0).astype(BF16)
            hi = jnp.where(lane >= GMLP_HEAD_DIM, vp, 0.0).astype(BF16)
            outs.append(_dot(wms[j], jnp.concatenate([lo, hi], axis=0)))
        mixed = jnp.concatenate(outs, axis=1) + bm_ref[...]
        yb_ref[0, r0:r0 + GMLP_CHUNK, :] = (gu[r0:r0 + GMLP_CHUNK] * mixed).astype(BF16)


def _even_mixer(h, g_mix, w_in, w_conv, g_v, w_s, b_s):
    bsz, seq, _ = h.shape
    gmat = jnp.asarray(np.kron(np.eye(GMLP_HEADS), np.ones((GMLP_HEAD_DIM, GMLP_HEAD_DIM))), BF16)
    wpair = w_s.reshape(GMLP_HEADS // 2, 2, GMLP_CHUNK, GMLP_CHUNK).transpose(0, 2, 1, 3)
    wpair = wpair.reshape(GMLP_HEADS // 2, GMLP_CHUNK, 2 * GMLP_CHUNK)
    bm = jnp.repeat(b_s.T, GMLP_HEAD_DIM, axis=1)
    out_sds = jax.ShapeDtypeStruct((bsz, seq, CONV_DIM), BF16)
    tile = lambda w: pl.BlockSpec((1, TM, w), lambda b, i: (b, i, 0))
    return pl.pallas_call(
        _even_kernel,
        grid=(bsz, seq // TM),
        in_specs=[tile(D_MODEL), _const_spec((1, D_MODEL)), _const_spec((D_MODEL, EVEN_IN)),
                  _const_spec((3, CONV_DIM)), _const_spec((GMLP_DIM, GMLP_DIM)), _const_spec((1, GMLP_DIM)),
                  _const_spec((GMLP_HEADS // 2, GMLP_CHUNK, 2 * GMLP_CHUNK)),
                  _const_spec((GMLP_CHUNK, GMLP_DIM))],
        out_specs=[tile(CONV_DIM), tile(GMLP_DIM)],
        out_shape=[out_sds, out_sds],
        scratch_shapes=[pltpu.VMEM((TM + 8, CONV_DIM), F32)],
        compiler_params=_params(2),
        name="even_mixer",
    )(h, g_mix.reshape(1, D_MODEL), w_in.astype(BF16), w_conv, gmat, g_v.reshape(1, GMLP_DIM), wpair, bm)


def _ffn_kernel(h_ref, y1_ref, y2_ref, p_ref, wout_ref, gffn_ref, wg_ref, wu_ref, wd_ref,
                gple_ref, wpg_ref, wpp_ref, o_ref):
    y = jnp.concatenate([y1_ref[0], y2_ref[0]], axis=-1)
    x = h_ref[0] + _dot(y, wout_ref[...])
    hn = _rms_rows(x, gffn_ref[...]).astype(BF16)
    acc = x
    for lo, hi in FF_CHUNKS:
        g = _dot(hn, wg_ref[:, lo:hi])
        u = _dot(hn, wu_ref[:, lo:hi])
        a = (g * jax.nn.sigmoid(g) * u).astype(BF16)
        acc = acc + _dot(a, wd_ref[lo:hi, :])
    hn2 = _rms_rows(acc, gple_ref[...]).astype(BF16)
    gate = jax.nn.sigmoid(_dot(hn2, wpg_ref[...]))
    pp = _dot(p_ref[0].astype(BF16), wpp_ref[...])
    o_ref[0] = acc + gate * pp


def _ffn_ple(h, y1, y2, p, w_out, g_ffn, w_gate, w_up, w_down, g_ple, w_pg, w_pp):
    bsz, seq, _ = h.shape
    tile = lambda w: pl.BlockSpec((1, TM, w), lambda b, i: (b, i, 0))
    return pl.pallas_call(
        _ffn_kernel,
        grid=(bsz, seq // TM),
        in_specs=[tile(D_MODEL), tile(512), tile(512), tile(D_PLE),
                  _const_spec((D_MODEL, D_MODEL)), _const_spec((1, D_MODEL)),
                  _const_spec((D_MODEL, D_FF)), _const_spec((D_MODEL, D_FF)), _const_spec((D_FF, D_MODEL)),
                  _const_spec((1, D_MODEL)), _const_spec((D_MODEL, D_MODEL)), _const_spec((D_PLE, D_MODEL))],
        out_specs=tile(D_MODEL),
        out_shape=jax.ShapeDtypeStruct((bsz, seq, D_MODEL), F32),
        compiler_params=_params(2),
        name="ffn_ple",
    )(h, y1, y2, p, w_out.astype(BF16), g_ffn.reshape(1, D_MODEL), w_gate.astype(BF16), w_up.astype(BF16),
      w_down.astype(BF16), g_ple.reshape(1, D_MODEL), w_pg.astype(BF16), w_pp.astype(BF16))


def _log_sigmoid(x):
    return jnp.minimum(x, 0.0) - jnp.log1p(jnp.exp(-jnp.abs(x)))


def _rope_cols(xr, cos, sin):
    half = MLA_ROPE // 2
    x1, x2 = xr[:half], xr[half:]
    return x1 * cos - x2 * sin, x2 * cos + x1 * sin


def _odd_kernel(h_ref, pos_ref, gmix_ref, win_ref, bias_ref, gqa_ref, gkva_ref, wqt_ref, wkvt_ref,
                gqn_ref, gqr_ref, gkn_ref, gkr_ref, freq_ref,
                qt_ref, k_ref, vt_ref, mq_ref, mkt_ref, mv_ref, mo_ref, g_ref, gt_ref):
    x = h_ref[0]
    hn = _rms_rows(x, gmix_ref[...]).astype(BF16)
    z = _dot(hn, win_ref[...])
    q_lat = z[:, 0:384]
    kv_lat = z[:, 384:640]

    mq_ref[0] = (z[:, 640:896] * (MLSTM_QK ** -0.5)).astype(BF16)
    mkt_ref[0] = z[:, 896:1152].T.astype(BF16)
    mv_ref[0] = z[:, 1152:1664].astype(BF16)
    mo_ref[0] = z[:, 1664:2176]
    tail = z[:, TAIL_OFF:ODD_IN_PAD] + bias_ref[...]
    lane = lax.broadcasted_iota(jnp.int32, (TM, LANES), 1)
    f_lane = (lane >= MLA_ROPE + MLSTM_HEADS) & (lane < MLA_ROPE + 2 * MLSTM_HEADS)
    tail = jnp.where(f_lane, _log_sigmoid(tail), tail)
    g_ref[0] = tail
    tail_t = tail.T
    gt_ref[0] = tail_t[MLA_ROPE:MLA_ROPE + 2 * MLSTM_HEADS]

    qn = _rms_rows(q_lat, gqa_ref[...]).astype(BF16)
    kvn = _rms_rows(kv_lat, gkva_ref[...]).astype(BF16)
    q_t = _dot_nt(wqt_ref[...], qn)
    kv_t = _dot_nt(wkvt_ref[...], kvn)

    ang = freq_ref[...] * pos_ref[0].astype(F32)
    cos = jnp.cos(ang)
    sin = jnp.sin(ang)
    kr1, kr2 = _rope_cols(_rms_cols(tail_t[0:MLA_ROPE], gkr_ref[...]), cos, sin)
    pad = jnp.zeros((HEAD_PAD - MLA_QK, TM), F32)
    scale = MLA_QK ** -0.5 * math.log2(math.e)
    for hd in range(MLA_HEADS):
        q0 = hd * MLA_QK
        q_nope = _rms_cols(q_t[q0:q0 + MLA_NOPE], gqn_ref[...])
        qr1, qr2 = _rope_cols(_rms_cols(q_t[q0 + MLA_NOPE:q0 + MLA_QK], gqr_ref[...]), cos, sin)
        q_full = jnp.concatenate([q_nope, qr1, qr2, pad], axis=0) * scale
        qt_ref[0, hd] = q_full.astype(BF16)
        k0 = hd * (MLA_NOPE + MLA_V)
        k_nope = _rms_cols(kv_t[k0:k0 + MLA_NOPE], gkn_ref[...])
        k_full = jnp.concatenate([k_nope, kr1, kr2, pad], axis=0)
        k_ref[0, hd] = k_full.T.astype(BF16)
        vt_ref[0, 0, hd] = kv_t[k0 + MLA_NOPE:k0 + MLA_NOPE + MLA_V].astype(BF16)


def _odd_proj(h, positions, g_mix, w_in, b_gate, g_qa, g_kva, w_q_up, w_kv_up, g_q, g_k):
    bsz, seq, _ = h.shape
    c = np.cumsum([0, Q_LORA, KV_LORA, MLA_ROPE, 256, 256, 512, 512, MLSTM_HEADS, MLSTM_HEADS])
    seg = lambda i: w_in[:, c[i]:c[i + 1]]
    tail_pad = jnp.zeros((D_MODEL, LANES - MLA_ROPE - 2 * MLSTM_HEADS), w_in.dtype)
    w_in_r = jnp.concatenate([seg(0), seg(1), seg(3), seg(4), seg(5), seg(6), seg(2), seg(7), seg(8), tail_pad],
                             axis=1).astype(BF16)
    bias = jnp.zeros((1, LANES), F32).at[0, MLA_ROPE:MLA_ROPE + 2 * MLSTM_HEADS].set(b_gate)
    inv_freq = ROPE_THETA ** (-jnp.arange(0, MLA_ROPE, 2, dtype=F32) / MLA_ROPE)
    col = lambda a: a.reshape(-1, 1)
    tile3 = lambda w: pl.BlockSpec((1, TM, w), lambda b, i: (b, i, 0))
    out_shape = [
        jax.ShapeDtypeStruct((bsz, MLA_HEADS, HEAD_PAD, seq), BF16),
        jax.ShapeDtypeStruct((bsz, MLA_HEADS, seq, HEAD_PAD), BF16),
        jax.ShapeDtypeStruct((bsz, seq // TK, MLA_HEADS, MLA_V, TK), BF16),
        jax.ShapeDtypeStruct((bsz, seq, 256), BF16),
        jax.ShapeDtypeStruct((bsz, 256, seq), BF16),
        jax.ShapeDtypeStruct((bsz, seq, 512), BF16),
        jax.ShapeDtypeStruct((bsz, seq, 512), F32),
        jax.ShapeDtypeStruct((bsz, seq, LANES), F32),
        jax.ShapeDtypeStruct((bsz, 2 * MLSTM_HEADS, seq), F32),
    ]
    out_specs = [
        pl.BlockSpec((1, MLA_HEADS, HEAD_PAD, TM), lambda b, i: (b, 0, 0, i)),
        pl.BlockSpec((1, MLA_HEADS, TM, HEAD_PAD), lambda b, i: (b, 0, i, 0)),
        pl.BlockSpec((1, 1, MLA_HEADS, MLA_V, TK), lambda b, i: (b, i, 0, 0, 0)),
        tile3(256),
        pl.BlockSpec((1, 256, TM), lambda b, i: (b, 0, i)),
        tile3(512), tile3(512), tile3(LANES),
        pl.BlockSpec((1, 2 * MLSTM_HEADS, TM), lambda b, i: (b, 0, i)),
    ]
    return pl.pallas_call(
        _odd_kernel,
        grid=(bsz, seq // TM),
        in_specs=[tile3(D_MODEL), pl.BlockSpec((1, 1, TM), lambda b, i: (b, 0, i)),
                  _const_spec((1, D_MODEL)), _const_spec((D_MODEL, ODD_IN_PAD)), _const_spec((1, LANES)),
                  _const_spec((1, Q_LORA)), _const_spec((1, KV_LORA)),
                  _const_spec((MLA_HEADS * MLA_QK, Q_LORA)), _const_spec((MLA_HEADS * (MLA_NOPE + MLA_V), KV_LORA)),
                  _const_spec((MLA_NOPE, 1)), _const_spec((MLA_ROPE, 1)),
                  _const_spec((MLA_NOPE, 1)), _const_spec((MLA_ROPE, 1)), _const_spec((MLA_ROPE // 2, 1))],
        out_specs=out_specs,
        out_shape=out_shape,
        compiler_params=_params(2),
        name="odd_proj",
    )(h, positions.reshape(bsz, 1, seq), g_mix.reshape(1, D_MODEL), w_in_r, bias,
      g_qa.reshape(1, Q_LORA), g_kva.reshape(1, KV_LORA), w_q_up.T.astype(BF16), w_kv_up.T.astype(BF16),
      col(g_q[:MLA_NOPE]), col(g_q[MLA_NOPE:]), col(g_k[:MLA_NOPE]), col(g_k[MLA_NOPE:]),
      inv_freq.reshape(-1, 1))


def _attn_kernel(qt_ref, k_ref, vt_ref, o_ref, m_ref, l_ref, acc_ref):
    qi = pl.program_id(1)
    m_ref[...] = jnp.full(m_ref.shape, -jnp.inf, F32)
    l_ref[...] = jnp.zeros(l_ref.shape, F32)
    acc_ref[...] = jnp.zeros(acc_ref.shape, F32)

    def scores(j, hd):
        k = k_ref[0, hd, pl.ds(pl.multiple_of(j * TK, TK), TK), :]
        return _dot(k, qt_ref[0, hd])

    def kv_tile(j, masked):
        s_next = scores(j, 0)
        for hd in range(MLA_HEADS):
            s = s_next
            if hd + 1 < MLA_HEADS:
                s_next = scores(j, hd + 1)
            if masked:
                row_s = lax.broadcasted_iota(jnp.int32, (TK, TQ), 0)
                col_t = lax.broadcasted_iota(jnp.int32, (TK, TQ), 1)
                s = jnp.where(row_s <= col_t, s, -jnp.inf)
            m_old = m_ref[hd]
            m_new = jnp.maximum(m_old, jnp.max(s, axis=0, keepdims=True))
            alpha = jnp.exp2(m_old - m_new)
            p = jnp.exp2(s - m_new)
            m_ref[hd] = m_new
            l_ref[hd] = alpha * l_ref[hd] + jnp.sum(p, axis=0, keepdims=True)
            acc_ref[hd] = alpha * acc_ref[hd] + _dot(vt_ref[0, j, hd], p.astype(BF16))

    def body(j, _):
        kv_tile(j, False)
        return 0

    lax.fori_loop(0, qi, body, 0)
    kv_tile(qi, True)
    out_t = jnp.concatenate([acc_ref[hd] / l_ref[hd] for hd in range(MLA_HEADS)], axis=0)
    o_ref[0] = out_t.T.astype(BF16)


def _attention(qt, k, vt):
    bsz, _, _, seq = qt.shape
    return pl.pallas_call(
        _attn_kernel,
        grid=(bsz, seq // TQ),
        in_specs=[pl.BlockSpec((1, MLA_HEADS, HEAD_PAD, TQ), lambda b, i: (b, 0, 0, i)),
                  pl.BlockSpec((1, MLA_HEADS, seq, HEAD_PAD), lambda b, i: (b, 0, 0, 0),
                               pipeline_mode=pl.Buffered(1)),
                  pl.BlockSpec((1, seq // TK, MLA_HEADS, MLA_V, TK), lambda b, i: (b, 0, 0, 0, 0),
                               pipeline_mode=pl.Buffered(1))],
        out_specs=pl.BlockSpec((1, TQ, MLA_HEADS * MLA_V), lambda b, i: (b, i, 0)),
        out_shape=jax.ShapeDtypeStruct((bsz, seq, MLA_HEADS * MLA_V), BF16),
        scratch_shapes=[pltpu.VMEM((MLA_HEADS, 1, TQ), F32), pltpu.VMEM((MLA_HEADS, 1, TQ), F32),
                        pltpu.VMEM((MLA_HEADS, MLA_V, TQ), F32)],
        compiler_params=_params(2),
        name="attention",
    )(qt, k, vt)


def _split3(x):
    hi = x.astype(BF16)
    r = x - hi.astype(F32)
    mid = r.astype(BF16)
    lo = (r - mid.astype(F32)).astype(BF16)
    return hi, mid, lo


def _mlstm_kernel(mq_ref, mkt_ref, mv_ref, mo_ref, g_ref, gt_ref, gmh_ref, yd_ref, ct_ref, m_ref):
    bsz = mq_ref.shape[0]

    @pl.when(pl.program_id(0) == 0)
    def _():
        ct_ref[...] = jnp.zeros(ct_ref.shape, F32)
        m_ref[...] = jnp.zeros(m_ref.shape, F32)

    row_t = lax.broadcasted_iota(jnp.int32, (ML, ML), 0)
    col_s = lax.broadcasted_iota(jnp.int32, (ML, ML), 1)
    causal = col_s <= row_t
    lower = jnp.where(causal, 1.0, 0.0).astype(BF16)
    upper = jnp.where(row_t <= col_s, 1.0, 0.0).astype(BF16)
    lane = lax.broadcasted_iota(jnp.int32, (ML, LANES), 1)
    ones_col = jnp.where(lane == 0, 1.0, 0.0).astype(BF16)
    q_lo = lane < MLSTM_QK

    for b in range(bsz):
        gcol = g_ref[b]
        grow = gt_ref[b]
        a_cols = sum(_dot(lower, piece) for piece in _split3(gcol))
        a_rows = sum(_dot(piece, upper) for piece in _split3(jnp.concatenate([grow, grow], axis=0)))
        for hd in range(MLSTM_HEADS):
            pair, half = hd // 2, hd % 2
            idx = b * MLSTM_HEADS + hd
            a_col = a_cols[:, MLA_ROPE + MLSTM_HEADS + hd:MLA_ROPE + MLSTM_HEADS + hd + 1]
            a_row = a_rows[MLSTM_HEADS + hd:MLSTM_HEADS + hd + 1]
            i_row = grow[hd:hd + 1]
            m_prev = m_ref[idx][0:1, 0:1]

            d = jnp.where(causal, a_col - a_row + i_row, -jnp.inf)
            inter = a_col + m_prev
            m_t = jnp.maximum(inter, jnp.max(d, axis=1, keepdims=True))
            w_intra = jnp.exp(d - m_t)
            w_inter = jnp.exp(inter - m_t)

            q_pair = mq_ref[b, :, pair * LANES:(pair + 1) * LANES]
            q_m = jnp.where(q_lo if half == 0 else ~q_lo, q_pair.astype(F32), 0.0).astype(BF16)
            kt_pair = mkt_ref[b, pair * LANES:(pair + 1) * LANES, :]
            s = _dot(q_m, kt_pair) * w_intra
            v_ext = jnp.concatenate([mv_ref[b, :, hd * LANES:(hd + 1) * LANES], ones_col], axis=1)
            nv = _dot(s.astype(BF16), v_ext)
            qc = _dot(q_m, ct_ref[b, pair].astype(BF16))
            num = nv[:, :MLSTM_V] + w_inter * qc[:, :MLSTM_V]
            den = jnp.sum(s, axis=1, keepdims=True) + w_inter * qc[:, MLSTM_V:MLSTM_V + 1]
            hh = num / jnp.maximum(jnp.abs(den), jnp.exp(-m_t))

            a_last = a_row[:, ML - 1:ML]
            m_new = m_t[ML - 1:ML, :]
            w_s = jnp.exp(a_last - a_row + i_row - m_new)
            decay = jnp.exp(a_last + m_prev - m_new)
            kt = mkt_ref[b, hd * MLSTM_QK:(hd + 1) * MLSTM_QK, :].astype(F32)
            upd = _dot((kt * w_s).astype(BF16), v_ext)
            r0 = half * MLSTM_QK
            ct_ref[b, pair, r0:r0 + MLSTM_QK, :] = decay * ct_ref[b, pair, r0:r0 + MLSTM_QK, :] + upd
            m_ref[idx] = jnp.broadcast_to(m_new, (8, LANES))

            hn = _rms_rows(hh, gmh_ref[:, hd * LANES:(hd + 1) * LANES])
            gate = jax.nn.sigmoid(mo_ref[b, :, hd * LANES:(hd + 1) * LANES])
            yd_ref[b, :, hd * LANES:(hd + 1) * LANES] = (gate * hn).astype(BF16)


def _mlstm(mq, mkt, mv, mo, g, gt, g_mh):
    bsz, seq, _ = mq.shape
    tile3 = lambda w: pl.BlockSpec((bsz, ML, w), lambda i: (0, i, 0))
    return pl.pallas_call(
        _mlstm_kernel,
        grid=(seq // ML,),
        in_specs=[tile3(256), pl.BlockSpec((bsz, 256, ML), lambda i: (0, 0, i)), tile3(512), tile3(512),
                  tile3(LANES), pl.BlockSpec((bsz, 2 * MLSTM_HEADS, ML), lambda i: (0, 0, i)),
                  _const_spec((1, MLSTM_HEADS * MLSTM_V))],
        out_specs=tile3(512),
        out_shape=jax.ShapeDtypeStruct((bsz, seq, MLSTM_HEADS * MLSTM_V), BF16),
        scratch_shapes=[pltpu.VMEM((bsz, MLSTM_HEADS // 2, 2 * MLSTM_QK, 2 * MLSTM_V), F32),
                        pltpu.VMEM((bsz * MLSTM_HEADS, 8, LANES), F32)],
        compiler_params=_params(1),
        name="mlstm",
    )(mq, mkt, mv, mo, g, gt, g_mh.reshape(1, MLSTM_HEADS * MLSTM_V))


def kernel(x, p, positions, g_mix, g_ffn, g_ple, ev_w_in, ev_w_conv, ev_g_v, ev_w_s, ev_b_s, ev_w_out,
           od_w_in, od_b_gate, od_g_qa, od_g_kva, od_w_q_up, od_w_kv_up, od_g_q, od_g_k, od_g_mh, od_w_out,
           w_gate, w_up, w_down, w_ple_proj, w_ple_gate):
    h = x
    depth = g_mix.shape[0]
    for layer in range(depth):
        j = layer // 2
        if layer % 2 == 0:
            y1, y2 = _even_mixer(h, g_mix[layer], ev_w_in[j], ev_w_conv[j], ev_g_v[j], ev_w_s[j], ev_b_s[j])
            w_out = ev_w_out[j]
        else:
            qt, k, vt, mq, mkt, mv, mo, g, gt = _odd_proj(
                h, positions, g_mix[layer], od_w_in[j], od_b_gate[j], od_g_qa[j], od_g_kva[j],
                od_w_q_up[j], od_w_kv_up[j], od_g_q[j], od_g_k[j])
            y1 = _attention(qt, k, vt)
            y2 = _mlstm(mq, mkt, mv, mo, g, gt, od_g_mh[j])
            w_out = od_w_out[j]
        h = _ffn_ple(h, y1, y2, p[layer], w_out, g_ffn[layer], w_gate[layer], w_up[layer], w_down[layer],
                     g_ple[layer], w_ple_gate[layer], w_ple_proj[layer])
    return h
```

```python
import functools
import math

import numpy as np
import jax
import jax.numpy as jnp
from jax import lax
from jax.experimental import pallas as pl
from jax.experimental.pallas import tpu as pltpu

F32 = jnp.float32
BF16 = jnp.bfloat16

D_MODEL = 1024
D_PLE = 256
EPS = 1e-6

CONV_DIM = 512
GMLP_DIM = 512
GMLP_HEADS = 8
GMLP_HEAD_DIM = 64
GMLP_CHUNK = 128
EVEN_IN = 3 * CONV_DIM + 2 * GMLP_DIM

MLA_HEADS = 8
MLA_NOPE = 64
MLA_ROPE = 32
MLA_V = 64
MLA_QK = MLA_NOPE + MLA_ROPE
Q_LORA = 384
KV_LORA = 256
ROPE_THETA = 10000.0
MLSTM_HEADS = 4
MLSTM_QK = 64
MLSTM_V = 128
D_FF = 2816

LANES = 128
MXU_EDGE = 256

TM = 512
TQ = 512
TK = 512
ML = 256
HEAD_PAD = 128
ODD_IN_PAD = 2304
TAIL_OFF = ODD_IN_PAD - LANES
FF_CHUNKS = ((0, 1024), (1024, 2048), (2048, 2816))
MAX_SCORE_SPREAD = 200.0

VMEM_LIMIT = 56 * 1024 * 1024


def _dot(a, b):
    return jnp.dot(a, b, preferred_element_type=F32)


def _dot_nt(a, b):
    return lax.dot_general(a, b, (((1,), (1,)), ((), ())), preferred_element_type=F32)


def _rms_rows(x, g):
    ms = jnp.mean(x * x, axis=-1, keepdims=True)
    return x * lax.rsqrt(ms + EPS) * g


def _rms_cols(x, g):
    ms = jnp.mean(x * x, axis=0, keepdims=True)
    return x * lax.rsqrt(ms + EPS) * g


def _const_spec(shape):
    zeros = (0,) * len(shape)
    return pl.BlockSpec(shape, lambda *_: zeros, pipeline_mode=pl.Buffered(1))


def _params(n_axes):
    return pltpu.CompilerParams(dimension_semantics=("arbitrary",) * n_axes,
                                vmem_limit_bytes=VMEM_LIMIT)


def _even_kernel(h_ref, gmix_ref, win_ref, wconv_ref, gmat_ref, gv_ref, wpair_ref, bm_ref,
                 ya_ref, yb_ref, zbuf_ref):
    @pl.when(pl.program_id(1) == 0)
    def _():
        zbuf_ref[0:8, :] = jnp.zeros((8, CONV_DIM), F32)

    x = h_ref[0]
    hn = _rms_rows(x, gmix_ref[...]).astype(BF16)
    z = _dot(hn, win_ref[...])
    b_gate = z[:, 0:512]
    c_gate = z[:, 512:1024]
    x_in = z[:, 1024:1536]
    u = z[:, 1536:2048]
    v = z[:, 2048:2560]

    zz = c_gate * x_in
    zbuf_ref[8:8 + TM, :] = zz
    z1 = zbuf_ref[7:7 + TM, :]
    z2 = zbuf_ref[6:6 + TM, :]
    wc = wconv_ref[...]
    conv = wc[2:3] * zz + wc[1:2] * z1 + wc[0:1] * z2
    ya_ref[0] = (b_gate * conv).astype(BF16)
    zbuf_ref[0:8, :] = zz[TM - 8:TM, :]

    gu = jax.nn.gelu(u)
    gv = jax.nn.gelu(v)
    ss = _dot((gv * gv).astype(BF16), gmat_ref[...])
    vn = gv * lax.rsqrt(ss * (1.0 / GMLP_HEAD_DIM) + EPS) * gv_ref[...]
    lane = lax.broadcasted_iota(jnp.int32, (GMLP_CHUNK, LANES), 1)
    row_t = lax.broadcasted_iota(jnp.int32, (GMLP_CHUNK, 2 * GMLP_CHUNK), 0)
    col_s = lax.broadcasted_iota(jnp.int32, (GMLP_CHUNK, 2 * GMLP_CHUNK), 1) % GMLP_CHUNK
    tril = col_s <= row_t
    wms = [jnp.where(tril, wpair_ref[j], 0.0).astype(BF16) for j in range(GMLP_HEADS // 2)]
    for c in range(TM // GMLP_CHUNK):
        r0 = c * GMLP_CHUNK
        outs = []
        for j in range(GMLP_HEADS // 2):
            vp = vn[r0:r0 + GMLP_CHUNK, j * LANES:(j + 1) * LANES]
            lo = jnp.where(lane < GMLP_HEAD_DIM, vp, 0.0).astype(BF16)
            hi = jnp.where(lane >= GMLP_HEAD_DIM, vp, 0.0).astype(BF16)
            outs.append(_dot(wms[j], jnp.concatenate([lo, hi], axis=0)))
        mixed = jnp.concatenate(outs, axis=1) + bm_ref[...]
        yb_ref[0, r0:r0 + GMLP_CHUNK, :] = (gu[r0:r0 + GMLP_CHUNK] * mixed).astype(BF16)


def _even_mixer(h, g_mix, w_in, w_conv, g_v, w_s, b_s):
    bsz, seq, _ = h.shape
    gmat = jnp.asarray(np.kron(np.eye(GMLP_HEADS), np.ones((GMLP_HEAD_DIM, GMLP_HEAD_DIM))), BF16)
    wpair = w_s.reshape(GMLP_HEADS // 2, 2, GMLP_CHUNK, GMLP_CHUNK).transpose(0, 2, 1, 3)
    wpair = wpair.reshape(GMLP_HEADS // 2, GMLP_CHUNK, 2 * GMLP_CHUNK)
    bm = jnp.repeat(b_s.T, GMLP_HEAD_DIM, axis=1)
    out_sds = jax.ShapeDtypeStruct((bsz, seq, CONV_DIM), BF16)
    tile = lambda w: pl.BlockSpec((1, TM, w), lambda b, i: (b, i, 0))
    return pl.pallas_call(
        _even_kernel,
        grid=(bsz, seq // TM),
        in_specs=[tile(D_MODEL), _const_spec((1, D_MODEL)), _const_spec((D_MODEL, EVEN_IN)),
                  _const_spec((3, CONV_DIM)), _const_spec((GMLP_DIM, GMLP_DIM)), _const_spec((1, GMLP_DIM)),
                  _const_spec((GMLP_HEADS // 2, GMLP_CHUNK, 2 * GMLP_CHUNK)),
                  _const_spec((GMLP_CHUNK, GMLP_DIM))],
        out_specs=[tile(CONV_DIM), tile(GMLP_DIM)],
        out_shape=[out_sds, out_sds],
        scratch_shapes=[pltpu.VMEM((TM + 8, CONV_DIM), F32)],
        compiler_params=_params(2),
        name="even_mixer",
    )(h, g_mix.reshape(1, D_MODEL), w_in.astype(BF16), w_conv, gmat, g_v.reshape(1, GMLP_DIM), wpair, bm)


def _ffn_kernel(h_ref, y1_ref, y2_ref, p_ref, wout_ref, gffn_ref, wg_ref, wu_ref, wd_ref,
                gple_ref, wpg_ref, wpp_ref, o_ref):
    y = jnp.concatenate([y1_ref[0], y2_ref[0]], axis=-1)
    x = h_ref[0] + _dot(y, wout_ref[...])
    hn = _rms_rows(x, gffn_ref[...]).astype(BF16)
    acc = x
    for lo, hi in FF_CHUNKS:
        g = _dot(hn, wg_ref[:, lo:hi])
        u = _dot(hn, wu_ref[:, lo:hi])
        a = (g * jax.nn.sigmoid(g) * u).astype(BF16)
        acc = acc + _dot(a, wd_ref[lo:hi, :])
    hn2 = _rms_rows(acc, gple_ref[...]).astype(BF16)
    gate = jax.nn.sigmoid(_dot(hn2, wpg_ref[...]))
    pp = _dot(p_ref[0].astype(BF16), wpp_ref[...])
    o_ref[0] = acc + gate * pp


def _ffn_ple(h, y1, y2, p, w_out, g_ffn, w_gate, w_up, w_down, g_ple, w_pg, w_pp):
    bsz, seq, _ = h.shape
    tile = lambda w: pl.BlockSpec((1, TM, w), lambda b, i: (b, i, 0))
    return pl.pallas_call(
        _ffn_kernel,
        grid=(bsz, seq // TM),
        in_specs=[tile(D_MODEL), tile(512), tile(512), tile(D_PLE),
                  _const_spec((D_MODEL, D_MODEL)), _const_spec((1, D_MODEL)),
                  _const_spec((D_MODEL, D_FF)), _const_spec((D_MODEL, D_FF)), _const_spec((D_FF, D_MODEL)),
                  _const_spec((1, D_MODEL)), _const_spec((D_MODEL, D_MODEL)), _const_spec((D_PLE, D_MODEL))],
        out_specs=tile(D_MODEL),
        out_shape=jax.ShapeDtypeStruct((bsz, seq, D_MODEL), F32),
        compiler_params=_params(2),
        name="ffn_ple",
    )(h, y1, y2, p, w_out.astype(BF16), g_ffn.reshape(1, D_MODEL), w_gate.astype(BF16), w_up.astype(BF16),
      w_down.astype(BF16), g_ple.reshape(1, D_MODEL), w_pg.astype(BF16), w_pp.astype(BF16))


def _log_sigmoid(x):
    return jnp.minimum(x, 0.0) - jnp.log1p(jnp.exp(-jnp.abs(x)))


def _rope_cols(xr, cos, sin):
    half = MLA_ROPE // 2
    x1, x2 = xr[:half], xr[half:]
    return x1 * cos - x2 * sin, x2 * cos + x1 * sin


def _odd_kernel(h_ref, pos_ref, gmix_ref, win_ref, bias_ref, gqa_ref, gkva_ref, wqt_ref, wkvt_ref,
                gqn_ref, gqr_ref, gkn_ref, gkr_ref, freq_ref,
                qt_ref, k_ref, vt_ref, spread_ref, mq_ref, mkt_ref, mv_ref, mo_ref, g_ref, gt_ref):
    x = h_ref[0]
    hn = _rms_rows(x, gmix_ref[...]).astype(BF16)
    z = _dot(hn, win_ref[...])
    q_lat = z[:, 0:384]
    kv_lat = z[:, 384:640]

    mq_ref[0] = (z[:, 640:896] * (MLSTM_QK ** -0.5)).astype(BF16)
    mkt_ref[0] = z[:, 896:1152].T.astype(BF16)
    mv_ref[0] = z[:, 1152:1664].astype(BF16)
    mo_ref[0] = z[:, 1664:2176]
    tail = z[:, TAIL_OFF:ODD_IN_PAD] + bias_ref[...]
    lane = lax.broadcasted_iota(jnp.int32, (TM, LANES), 1)
    f_lane = (lane >= MLA_ROPE + MLSTM_HEADS) & (lane < MLA_ROPE + 2 * MLSTM_HEADS)
    tail = jnp.where(f_lane, _log_sigmoid(tail), tail)
    g_ref[0] = tail
    tail_t = tail.T
    gt_ref[0] = tail_t[MLA_ROPE:MLA_ROPE + 2 * MLSTM_HEADS]

    qn = _rms_rows(q_lat, gqa_ref[...]).astype(BF16)
    kvn = _rms_rows(kv_lat, gkva_ref[...]).astype(BF16)
    q_t = _dot_nt(wqt_ref[...], qn)
    kv_t = _dot_nt(wkvt_ref[...], kvn)

    ang = freq_ref[...] * pos_ref[0].astype(F32)
    cos = jnp.cos(ang)
    sin = jnp.sin(ang)
    kr1, kr2 = _rope_cols(_rms_cols(tail_t[0:MLA_ROPE], gkr_ref[...]), cos, sin)
    scale = MLA_QK ** -0.5 * math.log2(math.e)

    k_bound = jnp.sqrt(MLA_NOPE * jnp.max(gkn_ref[...] ** 2, axis=0, keepdims=True)
                       + MLA_ROPE * jnp.max(gkr_ref[...] ** 2, axis=0, keepdims=True))
    first_row = lax.broadcasted_iota(jnp.int32, (8, TM), 0) == 0
    zeros_tail = jnp.zeros((HEAD_PAD - MLA_QK - 8, TM), F32)
    k_pad = jnp.concatenate([jnp.ones((8, TM), F32), zeros_tail], axis=0)
    spreads = []
    for hd in range(MLA_HEADS):
        q0 = hd * MLA_QK
        q_nope = _rms_cols(q_t[q0:q0 + MLA_NOPE], gqn_ref[...])
        qr1, qr2 = _rope_cols(_rms_cols(q_t[q0 + MLA_NOPE:q0 + MLA_QK], gqr_ref[...]), cos, sin)
        q_b = (jnp.concatenate([q_nope, qr1, qr2], axis=0) * scale).astype(BF16).astype(F32)
        k0 = hd * (MLA_NOPE + MLA_V)
        k_nope = _rms_cols(kv_t[k0:k0 + MLA_NOPE], gkn_ref[...])
        k_b = jnp.concatenate([k_nope, kr1, kr2], axis=0).astype(BF16).astype(F32)
        upper = jnp.sqrt(jnp.sum(q_b * q_b, axis=0, keepdims=True)) * k_bound
        lower = jnp.sum(q_b * k_b, axis=0, keepdims=True)
        shift = (0.5 * (upper + lower)).astype(BF16).astype(F32)
        spreads.append(upper - lower)
        q_full = jnp.concatenate([q_b, jnp.where(first_row, -shift, 0.0), zeros_tail], axis=0)
        qt_ref[0, hd] = q_full.astype(BF16)
        k_ref[0, hd] = jnp.concatenate([k_b, k_pad], axis=0).T.astype(BF16)
        vt_ref[0, 0, hd] = kv_t[k0 + MLA_NOPE:k0 + MLA_NOPE + MLA_V].astype(BF16)
    spread_ref[0] = jnp.concatenate(spreads, axis=0)


def _odd_proj(h, positions, g_mix, w_in, b_gate, g_qa, g_kva, w_q_up, w_kv_up, g_q, g_k):
    bsz, seq, _ = h.shape
    c = np.cumsum([0, Q_LORA, KV_LORA, MLA_ROPE, 256, 256, 512, 512, MLSTM_HEADS, MLSTM_HEADS])
    seg = lambda i: w_in[:, c[i]:c[i + 1]]
    tail_pad = jnp.zeros((D_MODEL, LANES - MLA_ROPE - 2 * MLSTM_HEADS), w_in.dtype)
    w_in_r = jnp.concatenate([seg(0), seg(1), seg(3), seg(4), seg(5), seg(6), seg(2), seg(7), seg(8), tail_pad],
                             axis=1).astype(BF16)
    bias = jnp.zeros((1, LANES), F32).at[0, MLA_ROPE:MLA_ROPE + 2 * MLSTM_HEADS].set(b_gate)
    inv_freq = ROPE_THETA ** (-jnp.arange(0, MLA_ROPE, 2, dtype=F32) / MLA_ROPE)
    col = lambda a: a.reshape(-1, 1)
    tile3 = lambda w: pl.BlockSpec((1, TM, w), lambda b, i: (b, i, 0))
    out_shape = [
        jax.ShapeDtypeStruct((bsz, MLA_HEADS, HEAD_PAD, seq), BF16),
        jax.ShapeDtypeStruct((bsz, MLA_HEADS, seq, HEAD_PAD), BF16),
        jax.ShapeDtypeStruct((bsz, seq // TK, MLA_HEADS, MLA_V, TK), BF16),
        jax.ShapeDtypeStruct((bsz, MLA_HEADS, seq), F32),
        jax.ShapeDtypeStruct((bsz, seq, 256), BF16),
        jax.ShapeDtypeStruct((bsz, 256, seq), BF16),
        jax.ShapeDtypeStruct((bsz, seq, 512), BF16),
        jax.ShapeDtypeStruct((bsz, seq, 512), F32),
        jax.ShapeDtypeStruct((bsz, seq, LANES), F32),
        jax.ShapeDtypeStruct((bsz, 2 * MLSTM_HEADS, seq), F32),
    ]
    out_specs = [
        pl.BlockSpec((1, MLA_HEADS, HEAD_PAD, TM), lambda b, i: (b, 0, 0, i)),
        pl.BlockSpec((1, MLA_HEADS, TM, HEAD_PAD), lambda b, i: (b, 0, i, 0)),
        pl.BlockSpec((1, 1, MLA_HEADS, MLA_V, TK), lambda b, i: (b, i, 0, 0, 0)),
        pl.BlockSpec((1, MLA_HEADS, TM), lambda b, i: (b, 0, i)),
        tile3(256),
        pl.BlockSpec((1, 256, TM), lambda b, i: (b, 0, i)),
        tile3(512), tile3(512), tile3(LANES),
        pl.BlockSpec((1, 2 * MLSTM_HEADS, TM), lambda b, i: (b, 0, i)),
    ]
    return pl.pallas_call(
        _odd_kernel,
        grid=(bsz, seq // TM),
        in_specs=[tile3(D_MODEL), pl.BlockSpec((1, 1, TM), lambda b, i: (b, 0, i)),
                  _const_spec((1, D_MODEL)), _const_spec((D_MODEL, ODD_IN_PAD)), _const_spec((1, LANES)),
                  _const_spec((1, Q_LORA)), _const_spec((1, KV_LORA)),
                  _const_spec((MLA_HEADS * MLA_QK, Q_LORA)), _const_spec((MLA_HEADS * (MLA_NOPE + MLA_V), KV_LORA)),
                  _const_spec((MLA_NOPE, 1)), _const_spec((MLA_ROPE, 1)),
                  _const_spec((MLA_NOPE, 1)), _const_spec((MLA_ROPE, 1)), _const_spec((MLA_ROPE // 2, 1))],
        out_specs=out_specs,
        out_shape=out_shape,
        compiler_params=_params(2),
        name="odd_proj",
    )(h, positions.reshape(bsz, 1, seq), g_mix.reshape(1, D_MODEL), w_in_r, bias,
      g_qa.reshape(1, Q_LORA), g_kva.reshape(1, KV_LORA), w_q_up.T.astype(BF16), w_kv_up.T.astype(BF16),
      col(g_q[:MLA_NOPE]), col(g_q[MLA_NOPE:]), col(g_k[:MLA_NOPE]), col(g_k[MLA_NOPE:]),
      inv_freq.reshape(-1, 1))


def _attn_kernel(qt_ref, k_ref, vt_ref, o_ref, m_ref, l_ref, acc_ref, *, running_max):
    qi = pl.program_id(1)
    m_ref[...] = jnp.full(m_ref.shape, -jnp.inf, F32)
    l_ref[...] = jnp.zeros(l_ref.shape, F32)
    acc_ref[...] = jnp.zeros(acc_ref.shape, F32)

    def scores(j, hd):
        k = k_ref[0, hd, pl.ds(pl.multiple_of(j * TK, TK), TK), :]
        return _dot(k, qt_ref[0, hd])

    def kv_tile(j, masked):
        s_next = scores(j, 0)
        for hd in range(MLA_HEADS):
            s = s_next
            if hd + 1 < MLA_HEADS:
                s_next = scores(j, hd + 1)
            if masked:
                row_s = lax.broadcasted_iota(jnp.int32, (TK, TQ), 0)
                col_t = lax.broadcasted_iota(jnp.int32, (TK, TQ), 1)
                s = jnp.where(row_s <= col_t, s, -jnp.inf)
            if running_max:
                m_old = m_ref[hd]
                m_new = jnp.maximum(m_old, jnp.max(s, axis=0, keepdims=True))
                alpha = jnp.exp2(m_old - m_new)
                p = jnp.exp2(s - m_new)
                m_ref[hd] = m_new
                l_ref[hd] = alpha * l_ref[hd] + jnp.sum(p, axis=0, keepdims=True)
                acc_ref[hd] = alpha * acc_ref[hd] + _dot(vt_ref[0, j, hd], p.astype(BF16))
            else:
                p = jnp.exp2(s)
                l_ref[hd] += jnp.sum(p, axis=0, keepdims=True)
                acc_ref[hd] += _dot(vt_ref[0, j, hd], p.astype(BF16))

    def body(j, _):
        kv_tile(j, False)
        return 0

    lax.fori_loop(0, qi, body, 0)
    kv_tile(qi, True)
    out_t = jnp.concatenate([acc_ref[hd] / l_ref[hd] for hd in range(MLA_HEADS)], axis=0)
    o_ref[0] = out_t.T.astype(BF16)


def _attention(qt, k, vt, *, running_max):
    bsz, _, _, seq = qt.shape
    return pl.pallas_call(
        functools.partial(_attn_kernel, running_max=running_max),
        grid=(bsz, seq // TQ),
        in_specs=[pl.BlockSpec((1, MLA_HEADS, HEAD_PAD, TQ), lambda b, i: (b, 0, 0, i)),
                  pl.BlockSpec((1, MLA_HEADS, seq, HEAD_PAD), lambda b, i: (b, 0, 0, 0),
                               pipeline_mode=pl.Buffered(1)),
                  pl.BlockSpec((1, seq // TK, MLA_HEADS, MLA_V, TK), lambda b, i: (b, 0, 0, 0, 0),
                               pipeline_mode=pl.Buffered(1))],
        out_specs=pl.BlockSpec((1, TQ, MLA_HEADS * MLA_V), lambda b, i: (b, i, 0)),
        out_shape=jax.ShapeDtypeStruct((bsz, seq, MLA_HEADS * MLA_V), BF16),
        scratch_shapes=[pltpu.VMEM((MLA_HEADS, 1, TQ), F32), pltpu.VMEM((MLA_HEADS, 1, TQ), F32),
                        pltpu.VMEM((MLA_HEADS, MLA_V, TQ), F32)],
        compiler_params=_params(2),
        name="attention_running_max" if running_max else "attention",
    )(qt, k, vt)


def _split3(x):
    hi = x.astype(BF16)
    r = x - hi.astype(F32)
    mid = r.astype(BF16)
    lo = (r - mid.astype(F32)).astype(BF16)
    return hi, mid, lo


def _mlstm_kernel(mq_ref, mkt_ref, mv_ref, mo_ref, g_ref, gt_ref, gmh_ref, yd_ref, ct_ref, m_ref):
    bsz = mq_ref.shape[0]

    @pl.when(pl.program_id(0) == 0)
    def _():
        ct_ref[...] = jnp.zeros(ct_ref.shape, F32)
        m_ref[...] = jnp.zeros(m_ref.shape, F32)

    row_t = lax.broadcasted_iota(jnp.int32, (ML, ML), 0)
    col_s = lax.broadcasted_iota(jnp.int32, (ML, ML), 1)
    causal = col_s <= row_t
    lower = jnp.where(causal, 1.0, 0.0).astype(BF16)
    upper = jnp.where(row_t <= col_s, 1.0, 0.0).astype(BF16)
    lane = lax.broadcasted_iota(jnp.int32, (ML, LANES), 1)
    ones_col = jnp.where(lane == 0, 1.0, 0.0).astype(BF16)
    q_lo = lane < MLSTM_QK

    for b in range(bsz):
        gcol = g_ref[b]
        grow = gt_ref[b]
        a_cols = sum(_dot(lower, piece) for piece in _split3(gcol))
        a_rows = sum(_dot(piece, upper) for piece in _split3(jnp.concatenate([grow, grow], axis=0)))
        for hd in range(MLSTM_HEADS):
            pair, half = hd // 2, hd % 2
            idx = b * MLSTM_HEADS + hd
            a_col = a_cols[:, MLA_ROPE + MLSTM_HEADS + hd:MLA_ROPE + MLSTM_HEADS + hd + 1]
            a_row = a_rows[MLSTM_HEADS + hd:MLSTM_HEADS + hd + 1]
            i_row = grow[hd:hd + 1]
            m_prev = m_ref[idx][0:1, 0:1]

            d = jnp.where(causal, a_col - a_row + i_row, -jnp.inf)
            inter = a_col + m_prev
            m_t = jnp.maximum(inter, jnp.max(d, axis=1, keepdims=True))
            w_intra = jnp.exp(d - m_t)
            w_inter = jnp.exp(inter - m_t)

            q_pair = mq_ref[b, :, pair * LANES:(pair + 1) * LANES]
            q_m = jnp.where(q_lo if half == 0 else ~q_lo, q_pair.astype(F32), 0.0).astype(BF16)
            kt_pair = mkt_ref[b, pair * LANES:(pair + 1) * LANES, :]
            s = _dot(q_m, kt_pair) * w_intra
            v_ext = jnp.concatenate([mv_ref[b, :, hd * LANES:(hd + 1) * LANES], ones_col], axis=1)
            nv = _dot(s.astype(BF16), v_ext)
            qc = _dot(q_m, ct_ref[b, pair].astype(BF16))
            num = nv[:, :MLSTM_V] + w_inter * qc[:, :MLSTM_V]
            den = jnp.sum(s, axis=1, keepdims=True) + w_inter * qc[:, MLSTM_V:MLSTM_V + 1]
            hh = num / jnp.maximum(jnp.abs(den), jnp.exp(-m_t))

            a_last = a_row[:, ML - 1:ML]
            m_new = m_t[ML - 1:ML, :]
            w_s = jnp.exp(a_last - a_row + i_row - m_new)
            decay = jnp.exp(a_last + m_prev - m_new)
            kt = mkt_ref[b, hd * MLSTM_QK:(hd + 1) * MLSTM_QK, :].astype(F32)
            upd = _dot((kt * w_s).astype(BF16), v_ext)
            r0 = half * MLSTM_QK
            ct_ref[b, pair, r0:r0 + MLSTM_QK, :] = decay * ct_ref[b, pair, r0:r0 + MLSTM_QK, :] + upd
            m_ref[idx] = jnp.broadcast_to(m_new, (8, LANES))

            hn = _rms_rows(hh, gmh_ref[:, hd * LANES:(hd + 1) * LANES])
            gate = jax.nn.sigmoid(mo_ref[b, :, hd * LANES:(hd + 1) * LANES])
            yd_ref[b, :, hd * LANES:(hd + 1) * LANES] = (gate * hn).astype(BF16)


def _mlstm(mq, mkt, mv, mo, g, gt, g_mh):
    bsz, seq, _ = mq.shape
    tile3 = lambda w: pl.BlockSpec((bsz, ML, w), lambda i: (0, i, 0))
    return pl.pallas_call(
        _mlstm_kernel,
        grid=(seq // ML,),
        in_specs=[tile3(256), pl.BlockSpec((bsz, 256, ML), lambda i: (0, 0, i)), tile3(512), tile3(512),
                  tile3(LANES), pl.BlockSpec((bsz, 2 * MLSTM_HEADS, ML), lambda i: (0, 0, i)),
                  _const_spec((1, MLSTM_HEADS * MLSTM_V))],
        out_specs=tile3(512),
        out_shape=jax.ShapeDtypeStruct((bsz, seq, MLSTM_HEADS * MLSTM_V), BF16),
        scratch_shapes=[pltpu.VMEM((bsz, MLSTM_HEADS // 2, 2 * MLSTM_QK, 2 * MLSTM_V), F32),
                        pltpu.VMEM((bsz * MLSTM_HEADS, 8, LANES), F32)],
        compiler_params=_params(1),
        name="mlstm",
    )(mq, mkt, mv, mo, g, gt, g_mh.reshape(1, MLSTM_HEADS * MLSTM_V))


def kernel(x, p, positions, g_mix, g_ffn, g_ple, ev_w_in, ev_w_conv, ev_g_v, ev_w_s, ev_b_s, ev_w_out,
           od_w_in, od_b_gate, od_g_qa, od_g_kva, od_w_q_up, od_w_kv_up, od_g_q, od_g_k, od_g_mh, od_w_out,
           w_gate, w_up, w_down, w_ple_proj, w_ple_gate):
    h = x
    depth = g_mix.shape[0]
    for layer in range(depth):
        j = layer // 2
        if layer % 2 == 0:
            y1, y2 = _even_mixer(h, g_mix[layer], ev_w_in[j], ev_w_conv[j], ev_g_v[j], ev_w_s[j], ev_b_s[j])
            w_out = ev_w_out[j]
        else:
            qt, k, vt, spread, mq, mkt, mv, mo, g, gt = _odd_proj(
                h, positions, g_mix[layer], od_w_in[j], od_b_gate[j], od_g_qa[j], od_g_kva[j],
                od_w_q_up[j], od_w_kv_up[j], od_g_q[j], od_g_k[j])
            y1 = lax.cond(jnp.max(spread) <= MAX_SCORE_SPREAD,
                          functools.partial(_attention, running_max=False),
                          functools.partial(_attention, running_max=True), qt, k, vt)
            y2 = _mlstm(mq, mkt, mv, mo, g, gt, od_g_mh[j])
            w_out = od_w_out[j]
        h = _ffn_ple(h, y1, y2, p[layer], w_out, g_ffn[layer], w_gate[layer], w_up[layer], w_down[layer],
                     g_ple[layer], w_ple_gate[layer], w_ple_proj[layer])
    return h
```

```python
import functools
import math

import numpy as np
import jax
import jax.numpy as jnp
from jax import lax
from jax.experimental import pallas as pl
from jax.experimental.pallas import tpu as pltpu

F32 = jnp.float32
BF16 = jnp.bfloat16

D_MODEL = 1024
D_PLE = 256
EPS = 1e-6

CONV_DIM = 512
GMLP_DIM = 512
GMLP_HEADS = 8
GMLP_HEAD_DIM = 64
GMLP_CHUNK = 128
EVEN_IN = 3 * CONV_DIM + 2 * GMLP_DIM

MLA_HEADS = 8
MLA_NOPE = 64
MLA_ROPE = 32
MLA_V = 64
MLA_QK = MLA_NOPE + MLA_ROPE
Q_LORA = 384
KV_LORA = 256
ROPE_THETA = 10000.0
MLSTM_HEADS = 4
MLSTM_QK = 64
MLSTM_V = 128
D_FF = 2816

LANES = 128
MXU_EDGE = 256

TM = 512
TQ = 512
TK = 512
KSUB = 256
LOOKAHEAD = 2
ML = 256
V_EXT = MLSTM_V + 16
HEAD_PAD = 128
ODD_IN_PAD = 2304
TAIL_OFF = ODD_IN_PAD - LANES
FF_CHUNKS = ((0, 1024), (1024, 2048), (2048, 2816))
MAX_SCORE_SPREAD = 200.0

VMEM_LIMIT = 56 * 1024 * 1024


def _dot(a, b):
    return jnp.dot(a, b, preferred_element_type=F32)


def _dot_nt(a, b):
    return lax.dot_general(a, b, (((1,), (1,)), ((), ())), preferred_element_type=F32)


def _rms_rows(x, g):
    ms = jnp.mean(x * x, axis=-1, keepdims=True)
    return x * lax.rsqrt(ms + EPS) * g


def _rms_cols(x, g):
    ms = jnp.mean(x * x, axis=0, keepdims=True)
    return x * lax.rsqrt(ms + EPS) * g


def _const_spec(shape):
    zeros = (0,) * len(shape)
    return pl.BlockSpec(shape, lambda *_: zeros, pipeline_mode=pl.Buffered(1))


def _params(n_axes):
    return pltpu.CompilerParams(dimension_semantics=("arbitrary",) * n_axes,
                                vmem_limit_bytes=VMEM_LIMIT)


def _even_kernel(h_ref, gmix_ref, win_ref, wconv_ref, gmat_ref, gv_ref, wpair_ref, bm_ref,
                 ya_ref, yb_ref, zbuf_ref):
    @pl.when(pl.program_id(1) == 0)
    def _():
        zbuf_ref[0:8, :] = jnp.zeros((8, CONV_DIM), F32)

    x = h_ref[0]
    hn = _rms_rows(x, gmix_ref[...]).astype(BF16)
    z = _dot(hn, win_ref[...])
    b_gate = z[:, 0:512]
    c_gate = z[:, 512:1024]
    x_in = z[:, 1024:1536]
    u = z[:, 1536:2048]
    v = z[:, 2048:2560]

    zz = c_gate * x_in
    zbuf_ref[8:8 + TM, :] = zz
    z1 = zbuf_ref[7:7 + TM, :]
    z2 = zbuf_ref[6:6 + TM, :]
    wc = wconv_ref[...]
    conv = wc[2:3] * zz + wc[1:2] * z1 + wc[0:1] * z2
    ya_ref[0] = (b_gate * conv).astype(BF16)
    zbuf_ref[0:8, :] = zz[TM - 8:TM, :]

    gu = jax.nn.gelu(u)
    gv = jax.nn.gelu(v)
    ss = _dot((gv * gv).astype(BF16), gmat_ref[...])
    vn = gv * lax.rsqrt(ss * (1.0 / GMLP_HEAD_DIM) + EPS) * gv_ref[...]
    lane = lax.broadcasted_iota(jnp.int32, (GMLP_CHUNK, LANES), 1)
    row_t = lax.broadcasted_iota(jnp.int32, (GMLP_CHUNK, 2 * GMLP_CHUNK), 0)
    col_s = lax.broadcasted_iota(jnp.int32, (GMLP_CHUNK, 2 * GMLP_CHUNK), 1) % GMLP_CHUNK
    tril = col_s <= row_t
    wms = [jnp.where(tril, wpair_ref[j], 0.0).astype(BF16) for j in range(GMLP_HEADS // 2)]
    for c in range(TM // GMLP_CHUNK):
        r0 = c * GMLP_CHUNK
        outs = []
        for j in range(GMLP_HEADS // 2):
            vp = vn[r0:r0 + GMLP_CHUNK, j * LANES:(j + 1) * LANES]
            lo = jnp.where(lane < GMLP_HEAD_DIM, vp, 0.0).astype(BF16)
            hi = jnp.where(lane >= GMLP_HEAD_DIM, vp, 0.0).astype(BF16)
            outs.append(_dot(wms[j], jnp.concatenate([lo, hi], axis=0)))
        mixed = jnp.concatenate(outs, axis=1) + bm_ref[...]
        yb_ref[0, r0:r0 + GMLP_CHUNK, :] = (gu[r0:r0 + GMLP_CHUNK] * mixed).astype(BF16)


def _even_mixer(h, g_mix, w_in, w_conv, g_v, w_s, b_s):
    bsz, seq, _ = h.shape
    gmat = jnp.asarray(np.kron(np.eye(GMLP_HEADS), np.ones((GMLP_HEAD_DIM, GMLP_HEAD_DIM))), BF16)
    wpair = w_s.reshape(GMLP_HEADS // 2, 2, GMLP_CHUNK, GMLP_CHUNK).transpose(0, 2, 1, 3)
    wpair = wpair.reshape(GMLP_HEADS // 2, GMLP_CHUNK, 2 * GMLP_CHUNK)
    bm = jnp.repeat(b_s.T, GMLP_HEAD_DIM, axis=1)
    out_sds = jax.ShapeDtypeStruct((bsz, seq, CONV_DIM), BF16)
    tile = lambda w: pl.BlockSpec((1, TM, w), lambda b, i: (b, i, 0))
    return pl.pallas_call(
        _even_kernel,
        grid=(bsz, seq // TM),
        in_specs=[tile(D_MODEL), _const_spec((1, D_MODEL)), _const_spec((D_MODEL, EVEN_IN)),
                  _const_spec((3, CONV_DIM)), _const_spec((GMLP_DIM, GMLP_DIM)), _const_spec((1, GMLP_DIM)),
                  _const_spec((GMLP_HEADS // 2, GMLP_CHUNK, 2 * GMLP_CHUNK)),
                  _const_spec((GMLP_CHUNK, GMLP_DIM))],
        out_specs=[tile(CONV_DIM), tile(GMLP_DIM)],
        out_shape=[out_sds, out_sds],
        scratch_shapes=[pltpu.VMEM((TM + 8, CONV_DIM), F32)],
        compiler_params=_params(2),
        name="even_mixer",
    )(h, g_mix.reshape(1, D_MODEL), w_in.astype(BF16), w_conv, gmat, g_v.reshape(1, GMLP_DIM), wpair, bm)


def _ffn_kernel(h_ref, y1_ref, y2_ref, p_ref, wout_ref, gffn_ref, wg_ref, wu_ref, wd_ref,
                gple_ref, wpg_ref, wpp_ref, o_ref):
    y = jnp.concatenate([y1_ref[0], y2_ref[0]], axis=-1)
    x = h_ref[0] + _dot(y, wout_ref[...])
    hn = _rms_rows(x, gffn_ref[...]).astype(BF16)
    acc = x
    for lo, hi in FF_CHUNKS:
        g = _dot(hn, wg_ref[:, lo:hi])
        u = _dot(hn, wu_ref[:, lo:hi])
        a = (g * jax.nn.sigmoid(g) * u).astype(BF16)
        acc = acc + _dot(a, wd_ref[lo:hi, :])
    hn2 = _rms_rows(acc, gple_ref[...]).astype(BF16)
    gate = jax.nn.sigmoid(_dot(hn2, wpg_ref[...]))
    pp = _dot(p_ref[0, 0].astype(BF16), wpp_ref[...])
    o_ref[0] = acc + gate * pp


def _ffn_ple(layer, h, y1, y2, p, w_out, g_ffn, g_ple, w_gate, w_up, w_down, w_pg, w_pp):
    bsz, seq, _ = h.shape
    tile = lambda w: pl.BlockSpec((1, TM, w), lambda b, i: (b, i, 0))
    layer_spec = lambda r, c: pl.BlockSpec((None, r, c), lambda *_: (layer, 0, 0), pipeline_mode=pl.Buffered(1))
    return pl.pallas_call(
        _ffn_kernel,
        grid=(bsz, seq // TM),
        in_specs=[tile(D_MODEL), tile(512), tile(512),
                  pl.BlockSpec((1, 1, TM, D_PLE), lambda b, i: (layer, b, i, 0)),
                  _const_spec((D_MODEL, D_MODEL)), _const_spec((1, D_MODEL)),
                  layer_spec(D_MODEL, D_FF), layer_spec(D_MODEL, D_FF), layer_spec(D_FF, D_MODEL),
                  _const_spec((1, D_MODEL)), layer_spec(D_MODEL, D_MODEL), layer_spec(D_PLE, D_MODEL)],
        out_specs=tile(D_MODEL),
        out_shape=jax.ShapeDtypeStruct((bsz, seq, D_MODEL), F32),
        compiler_params=_params(2),
        name="ffn_ple",
    )(h, y1, y2, p, w_out.astype(BF16), g_ffn.reshape(1, D_MODEL), w_gate, w_up, w_down,
      g_ple.reshape(1, D_MODEL), w_pg, w_pp)


def _log_sigmoid(x):
    return jnp.minimum(x, 0.0) - jnp.log1p(jnp.exp(-jnp.abs(x)))


def _rope_cols(xr, cos, sin):
    half = MLA_ROPE // 2
    x1, x2 = xr[:half], xr[half:]
    return x1 * cos - x2 * sin, x2 * cos + x1 * sin


def _odd_kernel(h_ref, pos_ref, gmix_ref, win_ref, bias_ref, gqa_ref, gkva_ref, wqt_ref, wkvt_ref,
                gqn_ref, gqr_ref, gkn_ref, gkr_ref, freq_ref,
                qt_ref, k_ref, vt_ref, spread_ref, mqt_ref, mk_ref, mvt_ref, mo_ref, g_ref, gt_ref):
    x = h_ref[0]
    hn = _rms_rows(x, gmix_ref[...]).astype(BF16)
    z = _dot(hn, win_ref[...])
    q_lat = z[:, 0:384]
    kv_lat = z[:, 384:640]

    mqt_ref[0] = (z[:, 640:896] * (MLSTM_QK ** -0.5)).T.astype(BF16)
    mk_ref[0] = z[:, 896:1152].astype(BF16)
    mvt_ref[0] = z[:, 1152:1664].T.astype(BF16)
    mo_ref[0] = z[:, 1664:2176]
    tail = z[:, TAIL_OFF:ODD_IN_PAD] + bias_ref[...]
    lane = lax.broadcasted_iota(jnp.int32, (TM, LANES), 1)
    f_lane = (lane >= MLA_ROPE + MLSTM_HEADS) & (lane < MLA_ROPE + 2 * MLSTM_HEADS)
    tail = jnp.where(f_lane, _log_sigmoid(tail), tail)
    g_ref[0] = tail
    tail_t = tail.T
    gt_ref[0] = tail_t[MLA_ROPE:MLA_ROPE + 2 * MLSTM_HEADS]

    qn = _rms_rows(q_lat, gqa_ref[...]).astype(BF16)
    kvn = _rms_rows(kv_lat, gkva_ref[...]).astype(BF16)
    q_t = _dot_nt(wqt_ref[...], qn)
    kv_t = _dot_nt(wkvt_ref[...], kvn)

    ang = freq_ref[...] * pos_ref[0].astype(F32)
    cos = jnp.cos(ang)
    sin = jnp.sin(ang)
    kr1, kr2 = _rope_cols(_rms_cols(tail_t[0:MLA_ROPE], gkr_ref[...]), cos, sin)
    scale = MLA_QK ** -0.5 * math.log2(math.e)

    k_bound = jnp.sqrt(MLA_NOPE * jnp.max(gkn_ref[...] ** 2, axis=0, keepdims=True)
                       + MLA_ROPE * jnp.max(gkr_ref[...] ** 2, axis=0, keepdims=True))
    first_row = lax.broadcasted_iota(jnp.int32, (8, TM), 0) == 0
    zeros_tail = jnp.zeros((HEAD_PAD - MLA_QK - 8, TM), F32)
    k_pad = jnp.concatenate([jnp.ones((8, TM), F32), zeros_tail], axis=0)
    spreads = []
    for hd in range(MLA_HEADS):
        q0 = hd * MLA_QK
        q_nope = _rms_cols(q_t[q0:q0 + MLA_NOPE], gqn_ref[...])
        qr1, qr2 = _rope_cols(_rms_cols(q_t[q0 + MLA_NOPE:q0 + MLA_QK], gqr_ref[...]), cos, sin)
        q_b = (jnp.concatenate([q_nope, qr1, qr2], axis=0) * scale).astype(BF16).astype(F32)
        k0 = hd * (MLA_NOPE + MLA_V)
        k_nope = _rms_cols(kv_t[k0:k0 + MLA_NOPE], gkn_ref[...])
        k_b = jnp.concatenate([k_nope, kr1, kr2], axis=0).astype(BF16).astype(F32)
        upper = jnp.sqrt(jnp.sum(q_b * q_b, axis=0, keepdims=True)) * k_bound
        lower = jnp.sum(q_b * k_b, axis=0, keepdims=True)
        shift = (0.5 * (upper + lower)).astype(BF16).astype(F32)
        spreads.append(upper - lower)
        q_full = jnp.concatenate([q_b, jnp.where(first_row, -shift, 0.0), zeros_tail], axis=0)
        qt_ref[0, hd] = q_full.astype(BF16)
        k_ref[0, hd] = jnp.concatenate([k_b, k_pad], axis=0).T.astype(BF16)
        vt_ref[0, 0, hd] = kv_t[k0 + MLA_NOPE:k0 + MLA_NOPE + MLA_V].astype(BF16)
    spread_ref[0] = jnp.concatenate(spreads, axis=0)


def _odd_proj(h, positions, g_mix, w_in, b_gate, g_qa, g_kva, w_q_up, w_kv_up, g_q, g_k):
    bsz, seq, _ = h.shape
    c = np.cumsum([0, Q_LORA, KV_LORA, MLA_ROPE, 256, 256, 512, 512, MLSTM_HEADS, MLSTM_HEADS])
    seg = lambda i: w_in[:, c[i]:c[i + 1]]
    tail_pad = jnp.zeros((D_MODEL, LANES - MLA_ROPE - 2 * MLSTM_HEADS), w_in.dtype)
    w_in_r = jnp.concatenate([seg(0), seg(1), seg(3), seg(4), seg(5), seg(6), seg(2), seg(7), seg(8), tail_pad],
                             axis=1).astype(BF16)
    bias = jnp.zeros((1, LANES), F32).at[0, MLA_ROPE:MLA_ROPE + 2 * MLSTM_HEADS].set(b_gate)
    inv_freq = ROPE_THETA ** (-jnp.arange(0, MLA_ROPE, 2, dtype=F32) / MLA_ROPE)
    col = lambda a: a.reshape(-1, 1)
    tile3 = lambda w: pl.BlockSpec((1, TM, w), lambda b, i: (b, i, 0))
    out_shape = [
        jax.ShapeDtypeStruct((bsz, MLA_HEADS, HEAD_PAD, seq), BF16),
        jax.ShapeDtypeStruct((bsz, MLA_HEADS, seq, HEAD_PAD), BF16),
        jax.ShapeDtypeStruct((bsz, seq // TK, MLA_HEADS, MLA_V, TK), BF16),
        jax.ShapeDtypeStruct((bsz, MLA_HEADS, seq), F32),
        jax.ShapeDtypeStruct((bsz, 256, seq), BF16),
        jax.ShapeDtypeStruct((bsz, seq, 256), BF16),
        jax.ShapeDtypeStruct((bsz, 512, seq), BF16),
        jax.ShapeDtypeStruct((bsz, seq, 512), F32),
        jax.ShapeDtypeStruct((bsz, seq, LANES), F32),
        jax.ShapeDtypeStruct((bsz, 2 * MLSTM_HEADS, seq), F32),
    ]
    out_specs = [
        pl.BlockSpec((1, MLA_HEADS, HEAD_PAD, TM), lambda b, i: (b, 0, 0, i)),
        pl.BlockSpec((1, MLA_HEADS, TM, HEAD_PAD), lambda b, i: (b, 0, i, 0)),
        pl.BlockSpec((1, 1, MLA_HEADS, MLA_V, TK), lambda b, i: (b, i, 0, 0, 0)),
        pl.BlockSpec((1, MLA_HEADS, TM), lambda b, i: (b, 0, i)),
        pl.BlockSpec((1, 256, TM), lambda b, i: (b, 0, i)),
        tile3(256),
        pl.BlockSpec((1, 512, TM), lambda b, i: (b, 0, i)),
        tile3(512), tile3(LANES),
        pl.BlockSpec((1, 2 * MLSTM_HEADS, TM), lambda b, i: (b, 0, i)),
    ]
    return pl.pallas_call(
        _odd_kernel,
        grid=(bsz, seq // TM),
        in_specs=[tile3(D_MODEL), pl.BlockSpec((1, 1, TM), lambda b, i: (b, 0, i)),
                  _const_spec((1, D_MODEL)), _const_spec((D_MODEL, ODD_IN_PAD)), _const_spec((1, LANES)),
                  _const_spec((1, Q_LORA)), _const_spec((1, KV_LORA)),
                  _const_spec((MLA_HEADS * MLA_QK, Q_LORA)), _const_spec((MLA_HEADS * (MLA_NOPE + MLA_V), KV_LORA)),
                  _const_spec((MLA_NOPE, 1)), _const_spec((MLA_ROPE, 1)),
                  _const_spec((MLA_NOPE, 1)), _const_spec((MLA_ROPE, 1)), _const_spec((MLA_ROPE // 2, 1))],
        out_specs=out_specs,
        out_shape=out_shape,
        compiler_params=_params(2),
        name="odd_proj",
    )(h, positions.reshape(bsz, 1, seq), g_mix.reshape(1, D_MODEL), w_in_r, bias,
      g_qa.reshape(1, Q_LORA), g_kva.reshape(1, KV_LORA), w_q_up.T.astype(BF16), w_kv_up.T.astype(BF16),
      col(g_q[:MLA_NOPE]), col(g_q[MLA_NOPE:]), col(g_k[:MLA_NOPE]), col(g_k[MLA_NOPE:]),
      inv_freq.reshape(-1, 1))


def _attn_kernel(qt_ref, k_ref, vt_ref, o_ref, m_ref, l_ref, acc_ref, *, running_max):
    qi = pl.program_id(1)
    m_ref[...] = jnp.full(m_ref.shape, -jnp.inf, F32)
    l_ref[...] = jnp.zeros(l_ref.shape, F32)
    acc_ref[...] = jnp.zeros(acc_ref.shape, F32)

    units = [(hd, part) for hd in range(MLA_HEADS) for part in range(TK // KSUB)]

    def scores(j, unit):
        hd, part = unit
        k = k_ref[0, hd, pl.ds(pl.multiple_of(j * TK + part * KSUB, KSUB), KSUB), :]
        return _dot(k, qt_ref[0, hd])

    def kv_tile(j, masked):
        pending = [scores(j, u) for u in units[:LOOKAHEAD]]
        for i, (hd, part) in enumerate(units):
            s = pending.pop(0)
            if i + LOOKAHEAD < len(units):
                pending.append(scores(j, units[i + LOOKAHEAD]))
            if masked:
                row_s = lax.broadcasted_iota(jnp.int32, (KSUB, TQ), 0) + part * KSUB
                col_t = lax.broadcasted_iota(jnp.int32, (KSUB, TQ), 1)
                s = jnp.where(row_s <= col_t, s, -jnp.inf)
            v_t = vt_ref[0, j, hd, :, part * KSUB:(part + 1) * KSUB]
            if running_max:
                m_old = m_ref[hd]
                m_new = jnp.maximum(m_old, jnp.max(s, axis=0, keepdims=True))
                alpha = jnp.exp2(m_old - m_new)
                p = jnp.exp2(s - m_new)
                m_ref[hd] = m_new
                l_ref[hd] = alpha * l_ref[hd] + jnp.sum(p, axis=0, keepdims=True)
                acc_ref[hd] = alpha * acc_ref[hd] + _dot(v_t, p.astype(BF16))
            else:
                p = jnp.exp2(s)
                l_ref[hd] += jnp.sum(p, axis=0, keepdims=True)
                acc_ref[hd] += _dot(v_t, p.astype(BF16))

    def body(j, _):
        kv_tile(j, False)
        return 0

    lax.fori_loop(0, qi, body, 0)
    kv_tile(qi, True)
    out_t = jnp.concatenate([acc_ref[hd] / l_ref[hd] for hd in range(MLA_HEADS)], axis=0)
    o_ref[0] = out_t.T.astype(BF16)


def _attention(qt, k, vt, *, running_max):
    bsz, _, _, seq = qt.shape
    return pl.pallas_call(
        functools.partial(_attn_kernel, running_max=running_max),
        grid=(bsz, seq // TQ),
        in_specs=[pl.BlockSpec((1, MLA_HEADS, HEAD_PAD, TQ), lambda b, i: (b, 0, 0, i)),
                  pl.BlockSpec((1, MLA_HEADS, seq, HEAD_PAD), lambda b, i: (b, 0, 0, 0),
                               pipeline_mode=pl.Buffered(1)),
                  pl.BlockSpec((1, seq // TK, MLA_HEADS, MLA_V, TK), lambda b, i: (b, 0, 0, 0, 0),
                               pipeline_mode=pl.Buffered(1))],
        out_specs=pl.BlockSpec((1, TQ, MLA_HEADS * MLA_V), lambda b, i: (b, i, 0)),
        out_shape=jax.ShapeDtypeStruct((bsz, seq, MLA_HEADS * MLA_V), BF16),
        scratch_shapes=[pltpu.VMEM((MLA_HEADS, 1, TQ), F32), pltpu.VMEM((MLA_HEADS, 1, TQ), F32),
                        pltpu.VMEM((MLA_HEADS, MLA_V, TQ), F32)],
        compiler_params=_params(2),
        name="attention_running_max" if running_max else "attention",
    )(qt, k, vt)


def _split3(x):
    hi = x.astype(BF16)
    r = x - hi.astype(F32)
    mid = r.astype(BF16)
    lo = (r - mid.astype(F32)).astype(BF16)
    return hi, mid, lo


def _mlstm_kernel(mqt_ref, mk_ref, mvt_ref, mo_ref, g_ref, gt_ref, gmh_ref, yd_ref, ct_ref, m_ref):
    bsz = mqt_ref.shape[0]

    @pl.when(pl.program_id(0) == 0)
    def _():
        ct_ref[...] = jnp.zeros(ct_ref.shape, F32)
        m_ref[...] = jnp.zeros(m_ref.shape, F32)

    row_s = lax.broadcasted_iota(jnp.int32, (ML, ML), 0)
    col_t = lax.broadcasted_iota(jnp.int32, (ML, ML), 1)
    visible = row_s <= col_t
    upper = jnp.where(visible, 1.0, 0.0).astype(BF16)
    lower = jnp.where(col_t <= row_s, 1.0, 0.0).astype(BF16)
    lane = lax.broadcasted_iota(jnp.int32, (ML, LANES), 1)
    k_lo = lane < MLSTM_QK
    ones_rows = jnp.where(lax.broadcasted_iota(jnp.int32, (V_EXT - MLSTM_V, ML), 0) == 0, 1.0, 0.0)
    i_lane0 = MLA_ROPE

    chains = [(b, hd) for b in range(bsz) for hd in range(MLSTM_HEADS)]
    a_row, i_row, c_col = {}, {}, {}
    for b in range(bsz):
        gcol = g_ref[b]
        grow = gt_ref[b]
        f_al = pltpu.roll(gcol, LANES - MLSTM_HEADS, axis=1)
        a_al = sum(_dot(lower, piece) for piece in _split3(f_al))
        a_rows = sum(_dot(piece, upper) for piece in _split3(jnp.concatenate([grow, grow], axis=0)))
        c_all = gcol - a_al
        for hd in range(MLSTM_HEADS):
            a_row[b, hd] = a_rows[MLSTM_HEADS + hd:MLSTM_HEADS + hd + 1]
            i_row[b, hd] = grow[hd:hd + 1]
            c_col[b, hd] = c_all[:, i_lane0 + hd:i_lane0 + hd + 1]

    k_m, qt_pair, vt_ext, sqk, qc = {}, {}, {}, {}, {}
    for b, hd in chains:
        pair, half = hd // 2, hd % 2
        k_pair = mk_ref[b, :, pair * LANES:(pair + 1) * LANES]
        k_m[b, hd] = jnp.where(k_lo if half == 0 else ~k_lo, k_pair.astype(F32), 0.0).astype(BF16)
        qt_pair[b, hd] = mqt_ref[b, pair * LANES:(pair + 1) * LANES, :]
        sqk[b, hd] = _dot(k_m[b, hd], qt_pair[b, hd])
    for b, hd in chains:
        idx = b * MLSTM_HEADS + hd
        qc[b, hd] = _dot(ct_ref[idx].astype(BF16), qt_pair[b, hd])

    m_t, m_new, m_prev, w_inter, s_t, den_intra = {}, {}, {}, {}, {}, {}
    for b, hd in chains:
        idx = b * MLSTM_HEADS + hd
        m_prev[b, hd] = m_ref[idx][0:1, 0:1]
        d_t = jnp.where(visible, a_row[b, hd] + c_col[b, hd], -jnp.inf)
        inter = a_row[b, hd] + m_prev[b, hd]
        m_t[b, hd] = jnp.maximum(inter, jnp.max(d_t, axis=0, keepdims=True))
        w_inter[b, hd] = jnp.exp(inter - m_t[b, hd])
        s_t[b, hd] = sqk[b, hd] * jnp.exp(d_t - m_t[b, hd])
        den_intra[b, hd] = jnp.sum(s_t[b, hd], axis=0, keepdims=True)

    nv = {}
    for b, hd in chains:
        v_t = mvt_ref[b, hd * MLSTM_V:(hd + 1) * MLSTM_V, :]
        vt_ext[b, hd] = jnp.concatenate([v_t.astype(F32), ones_rows], axis=0)
        nv[b, hd] = _dot(vt_ext[b, hd].astype(BF16), s_t[b, hd].astype(BF16))

    for b, hd in chains:
        idx = b * MLSTM_HEADS + hd
        a_last = a_row[b, hd][:, ML - 1:ML]
        m_new[b, hd] = m_t[b, hd][:, ML - 1:ML]
        w_s = jnp.exp(a_last - a_row[b, hd] + i_row[b, hd] - m_new[b, hd])
        decay = jnp.exp(a_last + m_prev[b, hd] - m_new[b, hd])
        upd = _dot((vt_ext[b, hd] * w_s).astype(BF16), k_m[b, hd])
        ct_ref[idx] = decay * ct_ref[idx] + upd
        m_ref[idx] = jnp.broadcast_to(m_new[b, hd], (8, LANES))

    for b in range(bsz):
        h_t = []
        for hd in range(MLSTM_HEADS):
            num = nv[b, hd][:MLSTM_V] + w_inter[b, hd] * qc[b, hd][:MLSTM_V]
            den = den_intra[b, hd] + w_inter[b, hd] * qc[b, hd][MLSTM_V:MLSTM_V + 1]
            hh = num / jnp.maximum(jnp.abs(den), jnp.exp(-m_t[b, hd]))
            h_t.append(hh * lax.rsqrt(jnp.mean(hh * hh, axis=0, keepdims=True) + EPS))
        hn = jnp.concatenate(h_t, axis=0).T * gmh_ref[...]
        yd_ref[b] = (jax.nn.sigmoid(mo_ref[b]) * hn).astype(BF16)


def _mlstm(mqt, mk, mvt, mo, g, gt, g_mh):
    bsz, seq, _ = mk.shape
    tile3 = lambda w: pl.BlockSpec((bsz, ML, w), lambda i: (0, i, 0))
    tile3t = lambda r: pl.BlockSpec((bsz, r, ML), lambda i: (0, 0, i))
    return pl.pallas_call(
        _mlstm_kernel,
        grid=(seq // ML,),
        in_specs=[tile3t(256), tile3(256), tile3t(512), tile3(512),
                  tile3(LANES), tile3t(2 * MLSTM_HEADS), _const_spec((1, MLSTM_HEADS * MLSTM_V))],
        out_specs=tile3(512),
        out_shape=jax.ShapeDtypeStruct((bsz, seq, MLSTM_HEADS * MLSTM_V), BF16),
        scratch_shapes=[pltpu.VMEM((bsz * MLSTM_HEADS, V_EXT, 2 * MLSTM_QK), F32),
                        pltpu.VMEM((bsz * MLSTM_HEADS, 8, LANES), F32)],
        compiler_params=_params(1),
        name="mlstm",
    )(mqt, mk, mvt, mo, g, gt, g_mh.reshape(1, MLSTM_HEADS * MLSTM_V))


def kernel(x, p, positions, g_mix, g_ffn, g_ple, ev_w_in, ev_w_conv, ev_g_v, ev_w_s, ev_b_s, ev_w_out,
           od_w_in, od_b_gate, od_g_qa, od_g_kva, od_w_q_up, od_w_kv_up, od_g_q, od_g_k, od_g_mh, od_w_out,
           w_gate, w_up, w_down, w_ple_proj, w_ple_gate):
    h = x
    depth = g_mix.shape[0]
    ffn_w = tuple(w.astype(BF16) for w in (w_gate, w_up, w_down, w_ple_gate, w_ple_proj))
    for layer in range(depth):
        j = layer // 2
        if layer % 2 == 0:
            y1, y2 = _even_mixer(h, g_mix[layer], ev_w_in[j], ev_w_conv[j], ev_g_v[j], ev_w_s[j], ev_b_s[j])
            w_out = ev_w_out[j]
        else:
            qt, k, vt, spread, mqt, mk, mvt, mo, g, gt = _odd_proj(
                h, positions, g_mix[layer], od_w_in[j], od_b_gate[j], od_g_qa[j], od_g_kva[j],
                od_w_q_up[j], od_w_kv_up[j], od_g_q[j], od_g_k[j])
            y1 = lax.cond(jnp.max(spread) <= MAX_SCORE_SPREAD,
                          functools.partial(_attention, running_max=False),
                          functools.partial(_attention, running_max=True), qt, k, vt)
            y2 = _mlstm(mqt, mk, mvt, mo, g, gt, od_g_mh[j])
            w_out = od_w_out[j]
        h = _ffn_ple(layer, h, y1, y2, p, w_out, g_ffn[layer], g_ple[layer], *ffn_w)
    return h
```

```python
import functools
import math

import numpy as np
import jax
import jax.numpy as jnp
from jax import lax
from jax.experimental import pallas as pl
from jax.experimental.pallas import tpu as pltpu

F32 = jnp.float32
BF16 = jnp.bfloat16

D_MODEL = 1024
D_PLE = 256
EPS = 1e-6

CONV_DIM = 512
GMLP_DIM = 512
GMLP_HEADS = 8
GMLP_HEAD_DIM = 64
GMLP_CHUNK = 128
EVEN_IN = 3 * CONV_DIM + 2 * GMLP_DIM

MLA_HEADS = 8
MLA_NOPE = 64
MLA_ROPE = 32
MLA_V = 64
MLA_QK = MLA_NOPE + MLA_ROPE
Q_LORA = 384
KV_LORA = 256
ROPE_THETA = 10000.0
MLSTM_HEADS = 4
MLSTM_QK = 64
MLSTM_V = 128
D_FF = 2816

LANES = 128
MXU_EDGE = 256

TM = 512
TM_FFN = 512
TQ = 512
TK = 512
KSUB = 256
LOOKAHEAD = 2
ML = 256
V_EXT = MLSTM_V + 16
HEAD_PAD = 128
ODD_IN_PAD = 2304
TAIL_OFF = ODD_IN_PAD - LANES
FF_CHUNKS = ((0, 1024), (1024, 2048), (2048, 2816))
MAX_SCORE_SPREAD = 200.0
SCORE_SCALE = MLA_QK ** -0.5 * math.log2(math.e)
BF16_SLACK = 1.01

VMEM_LIMIT = 56 * 1024 * 1024


def _dot(a, b):
    return jnp.dot(a, b, preferred_element_type=F32)


def _dot_nt(a, b):
    return lax.dot_general(a, b, (((1,), (1,)), ((), ())), preferred_element_type=F32)


def _rms_rows(x, g):
    ms = jnp.mean(x * x, axis=-1, keepdims=True)
    return x * lax.rsqrt(ms + EPS) * g


def _rms_cols(x, g):
    ms = jnp.mean(x * x, axis=0, keepdims=True)
    return x * lax.rsqrt(ms + EPS) * g


def _const_spec(shape):
    zeros = (0,) * len(shape)
    return pl.BlockSpec(shape, lambda *_: zeros, pipeline_mode=pl.Buffered(1))


def _params(n_axes):
    return pltpu.CompilerParams(dimension_semantics=("arbitrary",) * n_axes,
                                vmem_limit_bytes=VMEM_LIMIT)


def _even_kernel(h_ref, gmix_ref, win_ref, wconv_ref, gmat_ref, gv_ref, wpair_ref, bm_ref,
                 ya_ref, yb_ref, zbuf_ref):
    @pl.when(pl.program_id(1) == 0)
    def _():
        zbuf_ref[0:8, :] = jnp.zeros((8, CONV_DIM), F32)

    x = h_ref[0]
    hn = _rms_rows(x, gmix_ref[...]).astype(BF16)
    z_uv = _dot(hn, win_ref[:, 3 * CONV_DIM:EVEN_IN])
    u = z_uv[:, 0:GMLP_DIM]
    v = z_uv[:, GMLP_DIM:2 * GMLP_DIM]
    z_conv = _dot(hn, win_ref[:, 0:3 * CONV_DIM])
    b_gate = z_conv[:, 0:512]
    c_gate = z_conv[:, 512:1024]
    x_in = z_conv[:, 1024:1536]

    zz = c_gate * x_in
    zbuf_ref[8:8 + TM, :] = zz
    z1 = zbuf_ref[7:7 + TM, :]
    z2 = zbuf_ref[6:6 + TM, :]
    wc = wconv_ref[...]
    conv = wc[2:3] * zz + wc[1:2] * z1 + wc[0:1] * z2
    ya_ref[0] = (b_gate * conv).astype(BF16)
    zbuf_ref[0:8, :] = zz[TM - 8:TM, :]

    gu = jax.nn.gelu(u)
    gv = jax.nn.gelu(v)
    ss = _dot((gv * gv).astype(BF16), gmat_ref[...])
    vn = gv * lax.rsqrt(ss * (1.0 / GMLP_HEAD_DIM) + EPS) * gv_ref[...]
    lane = lax.broadcasted_iota(jnp.int32, (GMLP_CHUNK, LANES), 1)
    row_t = lax.broadcasted_iota(jnp.int32, (GMLP_CHUNK, 2 * GMLP_CHUNK), 0)
    col_s = lax.broadcasted_iota(jnp.int32, (GMLP_CHUNK, 2 * GMLP_CHUNK), 1) % GMLP_CHUNK
    tril = col_s <= row_t
    wms = [jnp.where(tril, wpair_ref[j], 0.0).astype(BF16) for j in range(GMLP_HEADS // 2)]
    for c in range(TM // GMLP_CHUNK):
        r0 = c * GMLP_CHUNK
        outs = []
        for j in range(GMLP_HEADS // 2):
            vp = vn[r0:r0 + GMLP_CHUNK, j * LANES:(j + 1) * LANES]
            lo = jnp.where(lane < GMLP_HEAD_DIM, vp, 0.0).astype(BF16)
            hi = jnp.where(lane >= GMLP_HEAD_DIM, vp, 0.0).astype(BF16)
            outs.append(_dot(wms[j], jnp.concatenate([lo, hi], axis=0)))
        mixed = jnp.concatenate(outs, axis=1) + bm_ref[...]
        yb_ref[0, r0:r0 + GMLP_CHUNK, :] = (gu[r0:r0 + GMLP_CHUNK] * mixed).astype(BF16)


def _even_mixer(h, g_mix, w_in, w_conv, g_v, w_s, b_s):
    bsz, seq, _ = h.shape
    gmat = jnp.asarray(np.kron(np.eye(GMLP_HEADS), np.ones((GMLP_HEAD_DIM, GMLP_HEAD_DIM))), BF16)
    wpair = w_s.reshape(GMLP_HEADS // 2, 2, GMLP_CHUNK, GMLP_CHUNK).transpose(0, 2, 1, 3)
    wpair = wpair.reshape(GMLP_HEADS // 2, GMLP_CHUNK, 2 * GMLP_CHUNK)
    bm = jnp.repeat(b_s.T, GMLP_HEAD_DIM, axis=1)
    out_sds = jax.ShapeDtypeStruct((bsz, seq, CONV_DIM), BF16)
    tile = lambda w: pl.BlockSpec((1, TM, w), lambda b, i: (b, i, 0))
    return pl.pallas_call(
        _even_kernel,
        grid=(bsz, seq // TM),
        in_specs=[tile(D_MODEL), _const_spec((1, D_MODEL)), _const_spec((D_MODEL, EVEN_IN)),
                  _const_spec((3, CONV_DIM)), _const_spec((GMLP_DIM, GMLP_DIM)), _const_spec((1, GMLP_DIM)),
                  _const_spec((GMLP_HEADS // 2, GMLP_CHUNK, 2 * GMLP_CHUNK)),
                  _const_spec((GMLP_CHUNK, GMLP_DIM))],
        out_specs=[tile(CONV_DIM), tile(GMLP_DIM)],
        out_shape=[out_sds, out_sds],
        scratch_shapes=[pltpu.VMEM((TM + 8, CONV_DIM), F32)],
        compiler_params=_params(2),
        name="even_mixer",
    )(h, g_mix.reshape(1, D_MODEL), w_in.astype(BF16), w_conv, gmat, g_v.reshape(1, GMLP_DIM), wpair, bm)


def _ffn_kernel(h_ref, y1_ref, y2_ref, p_ref, wout_ref, gffn_ref, wg_ref, wu_ref, wd_ref,
                gple_ref, wpg_ref, wpp_ref, o_ref):
    y = jnp.concatenate([y1_ref[0], y2_ref[0]], axis=-1)
    x = h_ref[0] + _dot(y, wout_ref[...])
    hn = _rms_rows(x, gffn_ref[...]).astype(BF16)
    acc = x
    for lo, hi in FF_CHUNKS:
        g = _dot(hn, wg_ref[:, lo:hi])
        u = _dot(hn, wu_ref[:, lo:hi])
        a = (g * jax.nn.sigmoid(g) * u).astype(BF16)
        acc = acc + _dot(a, wd_ref[lo:hi, :])
    hn2 = _rms_rows(acc, gple_ref[...]).astype(BF16)
    gate = jax.nn.sigmoid(_dot(hn2, wpg_ref[...]))
    pp = _dot(p_ref[0, 0].astype(BF16), wpp_ref[...])
    o_ref[0] = acc + gate * pp


def _ffn_ple(layer, h, y1, y2, p, w_out, g_ffn, g_ple, w_gate, w_up, w_down, w_pg, w_pp):
    bsz, seq, _ = h.shape
    tile = lambda w: pl.BlockSpec((1, TM_FFN, w), lambda b, i: (b, i, 0))
    layer_spec = lambda r, c: pl.BlockSpec((None, r, c), lambda *_: (layer, 0, 0), pipeline_mode=pl.Buffered(1))
    return pl.pallas_call(
        _ffn_kernel,
        grid=(bsz, seq // TM_FFN),
        in_specs=[tile(D_MODEL), tile(512), tile(512),
                  pl.BlockSpec((1, 1, TM_FFN, D_PLE), lambda b, i: (layer, b, i, 0)),
                  _const_spec((D_MODEL, D_MODEL)), _const_spec((1, D_MODEL)),
                  layer_spec(D_MODEL, D_FF), layer_spec(D_MODEL, D_FF), layer_spec(D_FF, D_MODEL),
                  _const_spec((1, D_MODEL)), layer_spec(D_MODEL, D_MODEL), layer_spec(D_PLE, D_MODEL)],
        out_specs=tile(D_MODEL),
        out_shape=jax.ShapeDtypeStruct((bsz, seq, D_MODEL), F32),
        compiler_params=_params(2),
        name="ffn_ple",
    )(h, y1, y2, p, w_out.astype(BF16), g_ffn.reshape(1, D_MODEL), w_gate, w_up, w_down,
      g_ple.reshape(1, D_MODEL), w_pg, w_pp)


def _log_sigmoid(x):
    return jnp.minimum(x, 0.0) - jnp.log1p(jnp.exp(-jnp.abs(x)))


def _rope_cols(xr, cos, sin):
    half = MLA_ROPE // 2
    x1, x2 = xr[:half], xr[half:]
    return x1 * cos - x2 * sin, x2 * cos + x1 * sin


def _odd_kernel(h_ref, pos_ref, gmix_ref, win_ref, bias_ref, gqa_ref, gkva_ref, wqt_ref, wkvt_ref,
                gqn_ref, gqr_ref, gkn_ref, gkr_ref, freq_ref,
                qt_ref, k_ref, vt_ref, spread_ref, mqt_ref, mk_ref, mvt_ref, mo_ref, g_ref, gt_ref):
    x = h_ref[0]
    hn = _rms_rows(x, gmix_ref[...]).astype(BF16)
    z = _dot(hn, win_ref[...])
    q_lat = z[:, 0:384]
    kv_lat = z[:, 384:640]

    mqt_ref[0] = (z[:, 640:896] * (MLSTM_QK ** -0.5)).T.astype(BF16)
    mk_ref[0] = z[:, 896:1152].astype(BF16)
    mvt_ref[0] = z[:, 1152:1664].T.astype(BF16)
    mo_ref[0] = z[:, 1664:2176]
    tail = z[:, TAIL_OFF:ODD_IN_PAD] + bias_ref[...]
    lane = lax.broadcasted_iota(jnp.int32, (TM, LANES), 1)
    f_lane = (lane >= MLA_ROPE + MLSTM_HEADS) & (lane < MLA_ROPE + 2 * MLSTM_HEADS)
    tail = jnp.where(f_lane, _log_sigmoid(tail), tail)
    g_ref[0] = tail
    tail_t = tail.T
    gt_ref[0] = tail_t[MLA_ROPE:MLA_ROPE + 2 * MLSTM_HEADS]

    qn = _rms_rows(q_lat, gqa_ref[...]).astype(BF16)
    kvn = _rms_rows(kv_lat, gkva_ref[...]).astype(BF16)
    q_t = _dot_nt(wqt_ref[...], qn)
    kv_t = _dot_nt(wkvt_ref[...], kvn)

    ang = freq_ref[...] * pos_ref[0].astype(F32)
    cos = jnp.cos(ang)
    sin = jnp.sin(ang)
    kr1, kr2 = _rope_cols(_rms_cols(tail_t[0:MLA_ROPE], gkr_ref[...]), cos, sin)

    norm_bound = lambda gn_ref, gr_ref: jnp.sqrt(
        MLA_NOPE * jnp.max(gn_ref[...] ** 2, axis=0, keepdims=True)
        + MLA_ROPE * jnp.max(gr_ref[...] ** 2, axis=0, keepdims=True))
    upper = BF16_SLACK * norm_bound(gqn_ref, gqr_ref) * norm_bound(gkn_ref, gkr_ref)
    first_row = lax.broadcasted_iota(jnp.int32, (8, TM), 0) == 0
    zeros_tail = jnp.zeros((HEAD_PAD - MLA_QK - 8, TM), F32)
    k_pad = jnp.concatenate([jnp.ones((8, TM), F32), zeros_tail], axis=0)
    spreads = []
    for hd in range(MLA_HEADS):
        q0 = hd * MLA_QK
        q_nope = _rms_cols(q_t[q0:q0 + MLA_NOPE], gqn_ref[...])
        qr1, qr2 = _rope_cols(_rms_cols(q_t[q0 + MLA_NOPE:q0 + MLA_QK], gqr_ref[...]), cos, sin)
        q96 = jnp.concatenate([q_nope, qr1, qr2], axis=0)
        k0 = hd * (MLA_NOPE + MLA_V)
        k_nope = _rms_cols(kv_t[k0:k0 + MLA_NOPE], gkn_ref[...])
        k96 = jnp.concatenate([k_nope, kr1, kr2], axis=0)
        lower = jnp.sum(q96 * k96, axis=0, keepdims=True)
        shift = 0.5 * (upper + lower)
        spreads.append(upper - lower)
        q_full = jnp.concatenate([q96, jnp.where(first_row, -shift, 0.0), zeros_tail], axis=0)
        qt_ref[0, hd] = q_full.astype(BF16)
        k_ref[0, hd] = jnp.concatenate([k96, k_pad], axis=0).T.astype(BF16)
        vt_ref[0, 0, hd] = kv_t[k0 + MLA_NOPE:k0 + MLA_NOPE + MLA_V].astype(BF16)
    spread_ref[0] = jnp.concatenate(spreads, axis=0)


def _odd_proj(h, positions, g_mix, w_in, b_gate, g_qa, g_kva, w_q_up, w_kv_up, g_q, g_k):
    bsz, seq, _ = h.shape
    c = np.cumsum([0, Q_LORA, KV_LORA, MLA_ROPE, 256, 256, 512, 512, MLSTM_HEADS, MLSTM_HEADS])
    seg = lambda i: w_in[:, c[i]:c[i + 1]]
    tail_pad = jnp.zeros((D_MODEL, LANES - MLA_ROPE - 2 * MLSTM_HEADS), w_in.dtype)
    w_in_r = jnp.concatenate([seg(0), seg(1), seg(3), seg(4), seg(5), seg(6), seg(2), seg(7), seg(8), tail_pad],
                             axis=1).astype(BF16)
    bias = jnp.zeros((1, LANES), F32).at[0, MLA_ROPE:MLA_ROPE + 2 * MLSTM_HEADS].set(b_gate)
    inv_freq = ROPE_THETA ** (-jnp.arange(0, MLA_ROPE, 2, dtype=F32) / MLA_ROPE)
    col = lambda a: a.reshape(-1, 1)
    tile3 = lambda w: pl.BlockSpec((1, TM, w), lambda b, i: (b, i, 0))
    out_shape = [
        jax.ShapeDtypeStruct((bsz, MLA_HEADS, HEAD_PAD, seq), BF16),
        jax.ShapeDtypeStruct((bsz, MLA_HEADS, seq, HEAD_PAD), BF16),
        jax.ShapeDtypeStruct((bsz, seq // TK, MLA_HEADS, MLA_V, TK), BF16),
        jax.ShapeDtypeStruct((bsz, MLA_HEADS, seq), F32),
        jax.ShapeDtypeStruct((bsz, 256, seq), BF16),
        jax.ShapeDtypeStruct((bsz, seq, 256), BF16),
        jax.ShapeDtypeStruct((bsz, 512, seq), BF16),
        jax.ShapeDtypeStruct((bsz, seq, 512), F32),
        jax.ShapeDtypeStruct((bsz, seq, LANES), F32),
        jax.ShapeDtypeStruct((bsz, 2 * MLSTM_HEADS, seq), F32),
    ]
    out_specs = [
        pl.BlockSpec((1, MLA_HEADS, HEAD_PAD, TM), lambda b, i: (b, 0, 0, i)),
        pl.BlockSpec((1, MLA_HEADS, TM, HEAD_PAD), lambda b, i: (b, 0, i, 0)),
        pl.BlockSpec((1, 1, MLA_HEADS, MLA_V, TK), lambda b, i: (b, i, 0, 0, 0)),
        pl.BlockSpec((1, MLA_HEADS, TM), lambda b, i: (b, 0, i)),
        pl.BlockSpec((1, 256, TM), lambda b, i: (b, 0, i)),
        tile3(256),
        pl.BlockSpec((1, 512, TM), lambda b, i: (b, 0, i)),
        tile3(512), tile3(LANES),
        pl.BlockSpec((1, 2 * MLSTM_HEADS, TM), lambda b, i: (b, 0, i)),
    ]
    return pl.pallas_call(
        _odd_kernel,
        grid=(bsz, seq // TM),
        in_specs=[tile3(D_MODEL), pl.BlockSpec((1, 1, TM), lambda b, i: (b, 0, i)),
                  _const_spec((1, D_MODEL)), _const_spec((D_MODEL, ODD_IN_PAD)), _const_spec((1, LANES)),
                  _const_spec((1, Q_LORA)), _const_spec((1, KV_LORA)),
                  _const_spec((MLA_HEADS * MLA_QK, Q_LORA)), _const_spec((MLA_HEADS * (MLA_NOPE + MLA_V), KV_LORA)),
                  _const_spec((MLA_NOPE, 1)), _const_spec((MLA_ROPE, 1)),
                  _const_spec((MLA_NOPE, 1)), _const_spec((MLA_ROPE, 1)), _const_spec((MLA_ROPE // 2, 1))],
        out_specs=out_specs,
        out_shape=out_shape,
        compiler_params=_params(2),
        name="odd_proj",
    )(h, positions.reshape(bsz, 1, seq), g_mix.reshape(1, D_MODEL), w_in_r, bias,
      g_qa.reshape(1, Q_LORA), g_kva.reshape(1, KV_LORA), w_q_up.T.astype(BF16), w_kv_up.T.astype(BF16),
      col(g_q[:MLA_NOPE] * SCORE_SCALE), col(g_q[MLA_NOPE:] * SCORE_SCALE),
      col(g_k[:MLA_NOPE]), col(g_k[MLA_NOPE:]),
      inv_freq.reshape(-1, 1))


def _attn_kernel(qt_ref, k_ref, vt_ref, o_ref, m_ref, l_ref, acc_ref, *, running_max):
    qi = pl.program_id(1)
    m_ref[...] = jnp.full(m_ref.shape, -jnp.inf, F32)
    l_ref[...] = jnp.zeros(l_ref.shape, F32)
    acc_ref[...] = jnp.zeros(acc_ref.shape, F32)

    units = [(hd, part) for hd in range(MLA_HEADS) for part in range(TK // KSUB)]

    def scores(j, unit, t0):
        hd, part = unit
        k = k_ref[0, hd, pl.ds(pl.multiple_of(j * TK + part * KSUB, KSUB), KSUB), :]
        return _dot(k, qt_ref[0, hd, :, t0:])

    def kv_tile(j, masked):
        first_q = lambda unit: unit[1] * KSUB if masked else 0
        pending = [scores(j, u, first_q(u)) for u in units[:LOOKAHEAD]]
        for i, (hd, part) in enumerate(units):
            s = pending.pop(0)
            if i + LOOKAHEAD < len(units):
                nxt = units[i + LOOKAHEAD]
                pending.append(scores(j, nxt, first_q(nxt)))
            t0 = first_q((hd, part))
            if masked:
                row_s = lax.broadcasted_iota(jnp.int32, s.shape, 0) + part * KSUB
                col_t = lax.broadcasted_iota(jnp.int32, s.shape, 1) + t0
                s = jnp.where(row_s <= col_t, s, -jnp.inf)
            v_t = vt_ref[0, j, hd, :, part * KSUB:(part + 1) * KSUB]
            if running_max:
                m_old = m_ref[hd, :, t0:]
                m_new = jnp.maximum(m_old, jnp.max(s, axis=0, keepdims=True))
                alpha = jnp.exp2(m_old - m_new)
                p = jnp.exp2(s - m_new)
                m_ref[hd, :, t0:] = m_new
                l_ref[hd, :, t0:] = alpha * l_ref[hd, :, t0:] + jnp.sum(p, axis=0, keepdims=True)
                acc_ref[hd, :, t0:] = alpha * acc_ref[hd, :, t0:] + _dot(v_t, p.astype(BF16))
            else:
                p = jnp.exp2(s)
                l_ref[hd, :, t0:] += jnp.sum(p, axis=0, keepdims=True)
                acc_ref[hd, :, t0:] += _dot(v_t, p.astype(BF16))

    def body(j, _):
        kv_tile(j, False)
        return 0

    lax.fori_loop(0, qi, body, 0)
    kv_tile(qi, True)
    out_t = jnp.concatenate([acc_ref[hd] / l_ref[hd] for hd in range(MLA_HEADS)], axis=0)
    o_ref[0] = out_t.T.astype(BF16)


def _attention(qt, k, vt, *, running_max):
    bsz, _, _, seq = qt.shape
    return pl.pallas_call(
        functools.partial(_attn_kernel, running_max=running_max),
        grid=(bsz, seq // TQ),
        in_specs=[pl.BlockSpec((1, MLA_HEADS, HEAD_PAD, TQ), lambda b, i: (b, 0, 0, i)),
                  pl.BlockSpec((1, MLA_HEADS, seq, HEAD_PAD), lambda b, i: (b, 0, 0, 0),
                               pipeline_mode=pl.Buffered(1)),
                  pl.BlockSpec((1, seq // TK, MLA_HEADS, MLA_V, TK), lambda b, i: (b, 0, 0, 0, 0),
                               pipeline_mode=pl.Buffered(1))],
        out_specs=pl.BlockSpec((1, TQ, MLA_HEADS * MLA_V), lambda b, i: (b, i, 0)),
        out_shape=jax.ShapeDtypeStruct((bsz, seq, MLA_HEADS * MLA_V), BF16),
        scratch_shapes=[pltpu.VMEM((MLA_HEADS, 1, TQ), F32), pltpu.VMEM((MLA_HEADS, 1, TQ), F32),
                        pltpu.VMEM((MLA_HEADS, MLA_V, TQ), F32)],
        compiler_params=_params(2),
        name="attention_running_max" if running_max else "attention",
    )(qt, k, vt)


def _split3(x):
    hi = x.astype(BF16)
    r = x - hi.astype(F32)
    mid = r.astype(BF16)
    lo = (r - mid.astype(F32)).astype(BF16)
    return hi, mid, lo


def _mlstm_kernel(mqt_ref, mk_ref, mvt_ref, mo_ref, g_ref, gt_ref, gmh_ref, yd_ref, ct_ref, m_ref):
    bsz = mqt_ref.shape[0]

    @pl.when(pl.program_id(0) == 0)
    def _():
        ct_ref[...] = jnp.zeros(ct_ref.shape, F32)
        m_ref[...] = jnp.zeros(m_ref.shape, F32)

    row_s = lax.broadcasted_iota(jnp.int32, (ML, ML), 0)
    col_t = lax.broadcasted_iota(jnp.int32, (ML, ML), 1)
    visible = row_s <= col_t
    upper = jnp.where(visible, 1.0, 0.0).astype(BF16)
    lower = jnp.where(col_t <= row_s, 1.0, 0.0).astype(BF16)
    lane = lax.broadcasted_iota(jnp.int32, (ML, LANES), 1)
    k_lo = lane < MLSTM_QK
    ones_rows = jnp.where(lax.broadcasted_iota(jnp.int32, (V_EXT - MLSTM_V, ML), 0) == 0, 1.0, 0.0)
    i_lane0 = MLA_ROPE

    chains = [(b, hd) for b in range(bsz) for hd in range(MLSTM_HEADS)]
    a_row, i_row, c_col = {}, {}, {}
    for b in range(bsz):
        gcol = g_ref[b]
        grow = gt_ref[b]
        f_al = pltpu.roll(gcol, LANES - MLSTM_HEADS, axis=1)
        a_al = sum(_dot(lower, piece) for piece in _split3(f_al))
        a_rows = sum(_dot(piece, upper) for piece in _split3(jnp.concatenate([grow, grow], axis=0)))
        c_all = gcol - a_al
        for hd in range(MLSTM_HEADS):
            a_row[b, hd] = a_rows[MLSTM_HEADS + hd:MLSTM_HEADS + hd + 1]
            i_row[b, hd] = grow[hd:hd + 1]
            c_col[b, hd] = c_all[:, i_lane0 + hd:i_lane0 + hd + 1]

    k_m, qt_pair, vt_ext, sqk, qc = {}, {}, {}, {}, {}
    for b, hd in chains:
        pair, half = hd // 2, hd % 2
        k_pair = mk_ref[b, :, pair * LANES:(pair + 1) * LANES]
        k_m[b, hd] = jnp.where(k_lo if half == 0 else ~k_lo, k_pair.astype(F32), 0.0).astype(BF16)
        qt_pair[b, hd] = mqt_ref[b, pair * LANES:(pair + 1) * LANES, :]
        sqk[b, hd] = _dot(k_m[b, hd], qt_pair[b, hd])
    for b, hd in chains:
        idx = b * MLSTM_HEADS + hd
        qc[b, hd] = _dot(ct_ref[idx].astype(BF16), qt_pair[b, hd])

    m_t, m_new, m_prev, w_inter, s_t, den_intra = {}, {}, {}, {}, {}, {}
    for b, hd in chains:
        idx = b * MLSTM_HEADS + hd
        m_prev[b, hd] = m_ref[idx][0:1, 0:1]
        d_t = jnp.where(visible, a_row[b, hd] + c_col[b, hd], -jnp.inf)
        inter = a_row[b, hd] + m_prev[b, hd]
        m_t[b, hd] = jnp.maximum(inter, jnp.max(d_t, axis=0, keepdims=True))
        w_inter[b, hd] = jnp.exp(inter - m_t[b, hd])
        s_t[b, hd] = sqk[b, hd] * jnp.exp(d_t - m_t[b, hd])
        den_intra[b, hd] = jnp.sum(s_t[b, hd], axis=0, keepdims=True)

    nv = {}
    for b, hd in chains:
        v_t = mvt_ref[b, hd * MLSTM_V:(hd + 1) * MLSTM_V, :]
        vt_ext[b, hd] = jnp.concatenate([v_t.astype(F32), ones_rows], axis=0)
        nv[b, hd] = _dot(vt_ext[b, hd].astype(BF16), s_t[b, hd].astype(BF16))

    for b, hd in chains:
        idx = b * MLSTM_HEADS + hd
        a_last = a_row[b, hd][:, ML - 1:ML]
        m_new[b, hd] = m_t[b, hd][:, ML - 1:ML]
        w_s = jnp.exp(a_last - a_row[b, hd] + i_row[b, hd] - m_new[b, hd])
        decay = jnp.exp(a_last + m_prev[b, hd] - m_new[b, hd])
        upd = _dot((vt_ext[b, hd] * w_s).astype(BF16), k_m[b, hd])
        ct_ref[idx] = decay * ct_ref[idx] + upd
        m_ref[idx] = jnp.broadcast_to(m_new[b, hd], (8, LANES))

    for b in range(bsz):
        h_t = []
        for hd in range(MLSTM_HEADS):
            num = nv[b, hd][:MLSTM_V] + w_inter[b, hd] * qc[b, hd][:MLSTM_V]
            den = den_intra[b, hd] + w_inter[b, hd] * qc[b, hd][MLSTM_V:MLSTM_V + 1]
            hh = num / jnp.maximum(jnp.abs(den), jnp.exp(-m_t[b, hd]))
            h_t.append(hh * lax.rsqrt(jnp.mean(hh * hh, axis=0, keepdims=True) + EPS))
        hn = jnp.concatenate(h_t, axis=0).T * gmh_ref[...]
        yd_ref[b] = (jax.nn.sigmoid(mo_ref[b]) * hn).astype(BF16)


def _mlstm(mqt, mk, mvt, mo, g, gt, g_mh):
    bsz, seq, _ = mk.shape
    tile3 = lambda w: pl.BlockSpec((bsz, ML, w), lambda i: (0, i, 0))
    tile3t = lambda r: pl.BlockSpec((bsz, r, ML), lambda i: (0, 0, i))
    return pl.pallas_call(
        _mlstm_kernel,
        grid=(seq // ML,),
        in_specs=[tile3t(256), tile3(256), tile3t(512), tile3(512),
                  tile3(LANES), tile3t(2 * MLSTM_HEADS), _const_spec((1, MLSTM_HEADS * MLSTM_V))],
        out_specs=tile3(512),
        out_shape=jax.ShapeDtypeStruct((bsz, seq, MLSTM_HEADS * MLSTM_V), BF16),
        scratch_shapes=[pltpu.VMEM((bsz * MLSTM_HEADS, V_EXT, 2 * MLSTM_QK), F32),
                        pltpu.VMEM((bsz * MLSTM_HEADS, 8, LANES), F32)],
        compiler_params=_params(1),
        name="mlstm",
    )(mqt, mk, mvt, mo, g, gt, g_mh.reshape(1, MLSTM_HEADS * MLSTM_V))


def kernel(x, p, positions, g_mix, g_ffn, g_ple, ev_w_in, ev_w_conv, ev_g_v, ev_w_s, ev_b_s, ev_w_out,
           od_w_in, od_b_gate, od_g_qa, od_g_kva, od_w_q_up, od_w_kv_up, od_g_q, od_g_k, od_g_mh, od_w_out,
           w_gate, w_up, w_down, w_ple_proj, w_ple_gate):
    h = x
    depth = g_mix.shape[0]
    ffn_w = tuple(w.astype(BF16) for w in (w_gate, w_up, w_down, w_ple_gate, w_ple_proj))
    for layer in range(depth):
        j = layer // 2
        if layer % 2 == 0:
            y1, y2 = _even_mixer(h, g_mix[layer], ev_w_in[j], ev_w_conv[j], ev_g_v[j], ev_w_s[j], ev_b_s[j])
            w_out = ev_w_out[j]
        else:
            qt, k, vt, spread, mqt, mk, mvt, mo, g, gt = _odd_proj(
                h, positions, g_mix[layer], od_w_in[j], od_b_gate[j], od_g_qa[j], od_g_kva[j],
                od_w_q_up[j], od_w_kv_up[j], od_g_q[j], od_g_k[j])
            y1 = lax.cond(jnp.max(spread) <= MAX_SCORE_SPREAD,
                          functools.partial(_attention, running_max=False),
                          functools.partial(_attention, running_max=True), qt, k, vt)
            y2 = _mlstm(mqt, mk, mvt, mo, g, gt, od_g_mh[j])
            w_out = od_w_out[j]
        h = _ffn_ple(layer, h, y1, y2, p, w_out, g_ffn[layer], g_ple[layer], *ffn_w)
    return h
```

```python
import functools
import math

import numpy as np
import jax
import jax.numpy as jnp
from jax import lax
from jax.experimental import pallas as pl
from jax.experimental.pallas import tpu as pltpu

F32 = jnp.float32
BF16 = jnp.bfloat16

D_MODEL = 1024
D_PLE = 256
EPS = 1e-6

CONV_DIM = 512
GMLP_DIM = 512
GMLP_HEADS = 8
GMLP_HEAD_DIM = 64
GMLP_CHUNK = 128
EVEN_IN = 3 * CONV_DIM + 2 * GMLP_DIM

MLA_HEADS = 8
MLA_NOPE = 64
MLA_ROPE = 32
MLA_V = 64
MLA_QK = MLA_NOPE + MLA_ROPE
Q_LORA = 384
KV_LORA = 256
ROPE_THETA = 10000.0
MLSTM_HEADS = 4
MLSTM_QK = 64
MLSTM_V = 128
D_FF = 2816

LANES = 128
MXU_EDGE = 256

TM = 512
TM_FFN = 512
TQ = 512
TK = 512
KSUB = 256
LOOKAHEAD = 2
ML = 256
V_EXT = MLSTM_V + 16
HEAD_PAD = 128
ODD_IN_PAD = 2304
TAIL_OFF = ODD_IN_PAD - LANES
FF_CHUNKS = ((0, 1024), (1024, 2048), (2048, 2816))
MAX_SCORE_SPREAD = 200.0
SCORE_SCALE = MLA_QK ** -0.5 * math.log2(math.e)
BF16_SLACK = 1.01

VMEM_LIMIT = 56 * 1024 * 1024


def _dot(a, b):
    return jnp.dot(a, b, preferred_element_type=F32)


def _dot_nt(a, b):
    return lax.dot_general(a, b, (((1,), (1,)), ((), ())), preferred_element_type=F32)


def _rms_rows(x, g):
    ms = jnp.mean(x * x, axis=-1, keepdims=True)
    return x * lax.rsqrt(ms + EPS) * g


def _rms_cols(x, g):
    ms = jnp.mean(x * x, axis=0, keepdims=True)
    return x * lax.rsqrt(ms + EPS) * g


def _const_spec(shape):
    zeros = (0,) * len(shape)
    return pl.BlockSpec(shape, lambda *_: zeros, pipeline_mode=pl.Buffered(1))


def _params(n_axes):
    return pltpu.CompilerParams(dimension_semantics=("arbitrary",) * n_axes,
                                vmem_limit_bytes=VMEM_LIMIT)


def _even_kernel(h_ref, gmix_ref, win_ref, wconv_ref, gmat_ref, gv_ref, wpair_ref, bm_ref,
                 ya_ref, yb_ref, zbuf_ref):
    @pl.when(pl.program_id(1) == 0)
    def _():
        zbuf_ref[0:8, :] = jnp.zeros((8, CONV_DIM), F32)

    x = h_ref[0]
    hn = _rms_rows(x, gmix_ref[...]).astype(BF16)
    z_uv = _dot(hn, win_ref[:, 3 * CONV_DIM:EVEN_IN])
    u = z_uv[:, 0:GMLP_DIM]
    v = z_uv[:, GMLP_DIM:2 * GMLP_DIM]
    z_conv = _dot(hn, win_ref[:, 0:3 * CONV_DIM])
    b_gate = z_conv[:, 0:512]
    c_gate = z_conv[:, 512:1024]
    x_in = z_conv[:, 1024:1536]

    zz = c_gate * x_in
    zbuf_ref[8:8 + TM, :] = zz
    z1 = zbuf_ref[7:7 + TM, :]
    z2 = zbuf_ref[6:6 + TM, :]
    wc = wconv_ref[...]
    conv = wc[2:3] * zz + wc[1:2] * z1 + wc[0:1] * z2
    ya_ref[0] = (b_gate * conv).astype(BF16)
    zbuf_ref[0:8, :] = zz[TM - 8:TM, :]

    gu = jax.nn.gelu(u)
    gv = jax.nn.gelu(v)
    ss = _dot((gv * gv).astype(BF16), gmat_ref[...])
    vn = gv * lax.rsqrt(ss * (1.0 / GMLP_HEAD_DIM) + EPS) * gv_ref[...]
    lane = lax.broadcasted_iota(jnp.int32, (GMLP_CHUNK, LANES), 1)
    row_t = lax.broadcasted_iota(jnp.int32, (GMLP_CHUNK, 2 * GMLP_CHUNK), 0)
    col_s = lax.broadcasted_iota(jnp.int32, (GMLP_CHUNK, 2 * GMLP_CHUNK), 1) % GMLP_CHUNK
    tril = col_s <= row_t
    wms = [jnp.where(tril, wpair_ref[j], 0.0).astype(BF16) for j in range(GMLP_HEADS // 2)]
    for c in range(TM // GMLP_CHUNK):
        r0 = c * GMLP_CHUNK
        outs = []
        for j in range(GMLP_HEADS // 2):
            vp = vn[r0:r0 + GMLP_CHUNK, j * LANES:(j + 1) * LANES]
            lo = jnp.where(lane < GMLP_HEAD_DIM, vp, 0.0).astype(BF16)
            hi = jnp.where(lane >= GMLP_HEAD_DIM, vp, 0.0).astype(BF16)
            outs.append(_dot(wms[j], jnp.concatenate([lo, hi], axis=0)))
        mixed = jnp.concatenate(outs, axis=1) + bm_ref[...]
        yb_ref[0, r0:r0 + GMLP_CHUNK, :] = (gu[r0:r0 + GMLP_CHUNK] * mixed).astype(BF16)


def _even_mixer(h, g_mix, w_in, w_conv, g_v, w_s, b_s):
    bsz, seq, _ = h.shape
    gmat = jnp.asarray(np.kron(np.eye(GMLP_HEADS), np.ones((GMLP_HEAD_DIM, GMLP_HEAD_DIM))), BF16)
    wpair = w_s.reshape(GMLP_HEADS // 2, 2, GMLP_CHUNK, GMLP_CHUNK).transpose(0, 2, 1, 3)
    wpair = wpair.reshape(GMLP_HEADS // 2, GMLP_CHUNK, 2 * GMLP_CHUNK)
    bm = jnp.repeat(b_s.T, GMLP_HEAD_DIM, axis=1)
    out_sds = jax.ShapeDtypeStruct((bsz, seq, CONV_DIM), BF16)
    tile = lambda w: pl.BlockSpec((1, TM, w), lambda b, i: (b, i, 0))
    return pl.pallas_call(
        _even_kernel,
        grid=(bsz, seq // TM),
        in_specs=[tile(D_MODEL), _const_spec((1, D_MODEL)), _const_spec((D_MODEL, EVEN_IN)),
                  _const_spec((3, CONV_DIM)), _const_spec((GMLP_DIM, GMLP_DIM)), _const_spec((1, GMLP_DIM)),
                  _const_spec((GMLP_HEADS // 2, GMLP_CHUNK, 2 * GMLP_CHUNK)),
                  _const_spec((GMLP_CHUNK, GMLP_DIM))],
        out_specs=[tile(CONV_DIM), tile(GMLP_DIM)],
        out_shape=[out_sds, out_sds],
        scratch_shapes=[pltpu.VMEM((TM + 8, CONV_DIM), F32)],
        compiler_params=_params(2),
        name="even_mixer",
    )(h, g_mix.reshape(1, D_MODEL), w_in.astype(BF16), w_conv, gmat, g_v.reshape(1, GMLP_DIM), wpair, bm)


def _ffn_kernel(h_ref, y1_ref, y2_ref, p_ref, wout_ref, gffn_ref, wg_ref, wu_ref, wd_ref,
                gple_ref, wpg_ref, wpp_ref, o_ref):
    y = jnp.concatenate([y1_ref[0], y2_ref[0]], axis=-1)
    x = h_ref[0] + _dot(y, wout_ref[...])
    hn = _rms_rows(x, gffn_ref[...]).astype(BF16)
    acc = x
    for lo, hi in FF_CHUNKS:
        g = _dot(hn, wg_ref[:, lo:hi])
        u = _dot(hn, wu_ref[:, lo:hi])
        a = (g * jax.nn.sigmoid(g) * u).astype(BF16)
        acc = acc + _dot(a, wd_ref[lo:hi, :])
    hn2 = _rms_rows(acc, gple_ref[...]).astype(BF16)
    gate = jax.nn.sigmoid(_dot(hn2, wpg_ref[...]))
    pp = _dot(p_ref[0, 0].astype(BF16), wpp_ref[...])
    o_ref[0] = acc + gate * pp


def _ffn_ple(layer, h, y1, y2, p, w_out, g_ffn, g_ple, w_gate, w_up, w_down, w_pg, w_pp):
    bsz, seq, _ = h.shape
    tile = lambda w: pl.BlockSpec((1, TM_FFN, w), lambda b, i: (b, i, 0))
    layer_spec = lambda r, c: pl.BlockSpec((None, r, c), lambda *_: (layer, 0, 0), pipeline_mode=pl.Buffered(1))
    return pl.pallas_call(
        _ffn_kernel,
        grid=(bsz, seq // TM_FFN),
        in_specs=[tile(D_MODEL), tile(512), tile(512),
                  pl.BlockSpec((1, 1, TM_FFN, D_PLE), lambda b, i: (layer, b, i, 0)),
                  _const_spec((D_MODEL, D_MODEL)), _const_spec((1, D_MODEL)),
                  layer_spec(D_MODEL, D_FF), layer_spec(D_MODEL, D_FF), layer_spec(D_FF, D_MODEL),
                  _const_spec((1, D_MODEL)), layer_spec(D_MODEL, D_MODEL), layer_spec(D_PLE, D_MODEL)],
        out_specs=tile(D_MODEL),
        out_shape=jax.ShapeDtypeStruct((bsz, seq, D_MODEL), F32),
        compiler_params=_params(2),
        name="ffn_ple",
    )(h, y1, y2, p, w_out.astype(BF16), g_ffn.reshape(1, D_MODEL), w_gate, w_up, w_down,
      g_ple.reshape(1, D_MODEL), w_pg, w_pp)


def _log_sigmoid(x):
    return jnp.minimum(x, 0.0) - jnp.log1p(jnp.exp(-jnp.abs(x)))


def _rope_cols(xr, cos, sin):
    half = MLA_ROPE // 2
    x1, x2 = xr[:half], xr[half:]
    return x1 * cos - x2 * sin, x2 * cos + x1 * sin


def _odd_kernel(h_ref, pos_ref, gmix_ref, win_ref, bias_ref, gqa_ref, gkva_ref, wqt_ref, wkvt_ref,
                gqn_ref, gqr_ref, gkn_ref, gkr_ref, freq_ref,
                qt_ref, k_ref, vt_ref, spread_ref, mqt_ref, mk_ref, mvt_ref, mo_ref, g_ref, gt_ref):
    x = h_ref[0]
    hn = _rms_rows(x, gmix_ref[...]).astype(BF16)
    z = _dot(hn, win_ref[...])
    q_lat = z[:, 0:384]
    kv_lat = z[:, 384:640]

    mqt_ref[0] = (z[:, 640:896] * (MLSTM_QK ** -0.5)).T.astype(BF16)
    mk_ref[0] = z[:, 896:1152].astype(BF16)
    mvt_ref[0] = z[:, 1152:1664].T.astype(BF16)
    mo_ref[0] = z[:, 1664:2176]
    tail = z[:, TAIL_OFF:ODD_IN_PAD] + bias_ref[...]
    lane = lax.broadcasted_iota(jnp.int32, (TM, LANES), 1)
    f_lane = (lane >= MLA_ROPE + MLSTM_HEADS) & (lane < MLA_ROPE + 2 * MLSTM_HEADS)
    tail = jnp.where(f_lane, _log_sigmoid(tail), tail)
    g_ref[0] = tail
    tail_t = tail.T
    gt_ref[0] = tail_t[MLA_ROPE:MLA_ROPE + 2 * MLSTM_HEADS]

    qn = _rms_rows(q_lat, gqa_ref[...]).astype(BF16)
    kvn = _rms_rows(kv_lat, gkva_ref[...]).astype(BF16)
    q_t = _dot_nt(wqt_ref[...], qn)
    kv_t = _dot_nt(wkvt_ref[...], kvn)

    ang = freq_ref[...] * pos_ref[0].astype(F32)
    cos = jnp.cos(ang)
    sin = jnp.sin(ang)
    kr1, kr2 = _rope_cols(_rms_cols(tail_t[0:MLA_ROPE], gkr_ref[...]), cos, sin)

    norm_bound = lambda gn_ref, gr_ref: jnp.sqrt(
        MLA_NOPE * jnp.max(gn_ref[...] ** 2, axis=0, keepdims=True)
        + MLA_ROPE * jnp.max(gr_ref[...] ** 2, axis=0, keepdims=True))
    upper = BF16_SLACK * norm_bound(gqn_ref, gqr_ref) * norm_bound(gkn_ref, gkr_ref)
    first_row = lax.broadcasted_iota(jnp.int32, (8, TM), 0) == 0
    zeros_tail = jnp.zeros((HEAD_PAD - MLA_QK - 8, TM), F32)
    k_pad = jnp.concatenate([jnp.ones((8, TM), F32), zeros_tail], axis=0)
    spreads = []
    for hd in range(MLA_HEADS):
        q0 = hd * MLA_QK
        q_nope = _rms_cols(q_t[q0:q0 + MLA_NOPE], gqn_ref[...])
        qr1, qr2 = _rope_cols(_rms_cols(q_t[q0 + MLA_NOPE:q0 + MLA_QK], gqr_ref[...]), cos, sin)
        q96 = jnp.concatenate([q_nope, qr1, qr2], axis=0)
        k0 = hd * (MLA_NOPE + MLA_V)
        k_nope = _rms_cols(kv_t[k0:k0 + MLA_NOPE], gkn_ref[...])
        k96 = jnp.concatenate([k_nope, kr1, kr2], axis=0)
        lower = jnp.sum(q96 * k96, axis=0, keepdims=True)
        shift = 0.5 * (upper + lower)
        spreads.append(upper - lower)
        q_full = jnp.concatenate([q96, jnp.where(first_row, -shift, 0.0), zeros_tail], axis=0)
        qt_ref[0, hd] = q_full.astype(BF16)
        k_ref[0, hd] = jnp.concatenate([k96, k_pad], axis=0).T.astype(BF16)
        vt_ref[0, 0, hd] = kv_t[k0 + MLA_NOPE:k0 + MLA_NOPE + MLA_V].astype(BF16)
    spread_ref[0] = jnp.concatenate(spreads, axis=0)


def _odd_proj(h, positions, g_mix, w_in, b_gate, g_qa, g_kva, w_q_up, w_kv_up, g_q, g_k):
    bsz, seq, _ = h.shape
    c = np.cumsum([0, Q_LORA + KV_LORA, MLA_ROPE, 2 * MLSTM_HEADS * (MLSTM_QK + MLSTM_V), 2 * MLSTM_HEADS])
    seg = lambda i: w_in[:, c[i]:c[i + 1]].astype(BF16)
    w_in_r = jnp.zeros((D_MODEL, ODD_IN_PAD), BF16)
    for col0, i in ((0, 0), (c[1], 2), (TAIL_OFF, 1), (TAIL_OFF + MLA_ROPE, 3)):
        w_in_r = lax.dynamic_update_slice(w_in_r, seg(i), (0, col0))
    bias = jnp.zeros((1, LANES), F32).at[0, MLA_ROPE:MLA_ROPE + 2 * MLSTM_HEADS].set(b_gate)
    inv_freq = ROPE_THETA ** (-jnp.arange(0, MLA_ROPE, 2, dtype=F32) / MLA_ROPE)
    col = lambda a: a.reshape(-1, 1)
    tile3 = lambda w: pl.BlockSpec((1, TM, w), lambda b, i: (b, i, 0))
    out_shape = [
        jax.ShapeDtypeStruct((bsz, MLA_HEADS, HEAD_PAD, seq), BF16),
        jax.ShapeDtypeStruct((bsz, MLA_HEADS, seq, HEAD_PAD), BF16),
        jax.ShapeDtypeStruct((bsz, seq // TK, MLA_HEADS, MLA_V, TK), BF16),
        jax.ShapeDtypeStruct((bsz, MLA_HEADS, seq), F32),
        jax.ShapeDtypeStruct((bsz, 256, seq), BF16),
        jax.ShapeDtypeStruct((bsz, seq, 256), BF16),
        jax.ShapeDtypeStruct((bsz, 512, seq), BF16),
        jax.ShapeDtypeStruct((bsz, seq, 512), F32),
        jax.ShapeDtypeStruct((bsz, seq, LANES), F32),
        jax.ShapeDtypeStruct((bsz, 2 * MLSTM_HEADS, seq), F32),
    ]
    out_specs = [
        pl.BlockSpec((1, MLA_HEADS, HEAD_PAD, TM), lambda b, i: (b, 0, 0, i)),
        pl.BlockSpec((1, MLA_HEADS, TM, HEAD_PAD), lambda b, i: (b, 0, i, 0)),
        pl.BlockSpec((1, 1, MLA_HEADS, MLA_V, TK), lambda b, i: (b, i, 0, 0, 0)),
        pl.BlockSpec((1, MLA_HEADS, TM), lambda b, i: (b, 0, i)),
        pl.BlockSpec((1, 256, TM), lambda b, i: (b, 0, i)),
        tile3(256),
        pl.BlockSpec((1, 512, TM), lambda b, i: (b, 0, i)),
        tile3(512), tile3(LANES),
        pl.BlockSpec((1, 2 * MLSTM_HEADS, TM), lambda b, i: (b, 0, i)),
    ]
    return pl.pallas_call(
        _odd_kernel,
        grid=(bsz, seq // TM),
        in_specs=[tile3(D_MODEL), pl.BlockSpec((1, 1, TM), lambda b, i: (b, 0, i)),
                  _const_spec((1, D_MODEL)), _const_spec((D_MODEL, ODD_IN_PAD)), _const_spec((1, LANES)),
                  _const_spec((1, Q_LORA)), _const_spec((1, KV_LORA)),
                  _const_spec((MLA_HEADS * MLA_QK, Q_LORA)), _const_spec((MLA_HEADS * (MLA_NOPE + MLA_V), KV_LORA)),
                  _const_spec((MLA_NOPE, 1)), _const_spec((MLA_ROPE, 1)),
                  _const_spec((MLA_NOPE, 1)), _const_spec((MLA_ROPE, 1)), _const_spec((MLA_ROPE // 2, 1))],
        out_specs=out_specs,
        out_shape=out_shape,
        compiler_params=_params(2),
        name="odd_proj",
    )(h, positions.reshape(bsz, 1, seq), g_mix.reshape(1, D_MODEL), w_in_r, bias,
      g_qa.reshape(1, Q_LORA), g_kva.reshape(1, KV_LORA), w_q_up.T.astype(BF16), w_kv_up.T.astype(BF16),
      col(g_q[:MLA_NOPE] * SCORE_SCALE), col(g_q[MLA_NOPE:] * SCORE_SCALE),
      col(g_k[:MLA_NOPE]), col(g_k[MLA_NOPE:]),
      inv_freq.reshape(-1, 1))


def _attn_kernel(qt_ref, k_ref, vt_ref, o_ref, m_ref, l_ref, acc_ref, kbuf_ref, vbuf_ref, *, running_max):
    qi = pl.program_id(1)
    m_ref[...] = jnp.full(m_ref.shape, -jnp.inf, F32)
    l_ref[...] = jnp.zeros(l_ref.shape, F32)
    acc_ref[...] = jnp.zeros(acc_ref.shape, F32)
    kbuf_ref[qi] = k_ref[0]
    vbuf_ref[qi] = vt_ref[0, 0]

    units = [(hd, part) for hd in range(MLA_HEADS) for part in range(TK // KSUB)]

    def scores(j, unit, t0):
        hd, part = unit
        k = kbuf_ref[j, hd, part * KSUB:(part + 1) * KSUB, :]
        return _dot(k, qt_ref[0, hd, :, t0:])

    def kv_tile(j, masked):
        first_q = lambda unit: unit[1] * KSUB if masked else 0
        pending = [scores(j, u, first_q(u)) for u in units[:LOOKAHEAD]]
        for i, (hd, part) in enumerate(units):
            s = pending.pop(0)
            if i + LOOKAHEAD < len(units):
                nxt = units[i + LOOKAHEAD]
                pending.append(scores(j, nxt, first_q(nxt)))
            t0 = first_q((hd, part))
            if masked:
                row_s = lax.broadcasted_iota(jnp.int32, s.shape, 0) + part * KSUB
                col_t = lax.broadcasted_iota(jnp.int32, s.shape, 1) + t0
                s = jnp.where(row_s <= col_t, s, -jnp.inf)
            v_t = vbuf_ref[j, hd, :, part * KSUB:(part + 1) * KSUB]
            if running_max:
                m_old = m_ref[hd, :, t0:]
                m_new = jnp.maximum(m_old, jnp.max(s, axis=0, keepdims=True))
                alpha = jnp.exp2(m_old - m_new)
                p = jnp.exp2(s - m_new)
                m_ref[hd, :, t0:] = m_new
                l_ref[hd, :, t0:] = alpha * l_ref[hd, :, t0:] + jnp.sum(p, axis=0, keepdims=True)
                acc_ref[hd, :, t0:] = alpha * acc_ref[hd, :, t0:] + _dot(v_t, p.astype(BF16))
            else:
                p = jnp.exp2(s)
                l_ref[hd, :, t0:] += jnp.sum(p, axis=0, keepdims=True)
                acc_ref[hd, :, t0:] += _dot(v_t, p.astype(BF16))

    def body(j, _):
        kv_tile(j, False)
        return 0

    lax.fori_loop(0, qi, body, 0)
    kv_tile(qi, True)
    out_t = jnp.concatenate([acc_ref[hd] / l_ref[hd] for hd in range(MLA_HEADS)], axis=0)
    o_ref[0] = out_t.T.astype(BF16)


def _attention(qt, k, vt, *, running_max):
    bsz, _, _, seq = qt.shape
    return pl.pallas_call(
        functools.partial(_attn_kernel, running_max=running_max),
        grid=(bsz, seq // TQ),
        in_specs=[pl.BlockSpec((1, MLA_HEADS, HEAD_PAD, TQ), lambda b, i: (b, 0, 0, i)),
                  pl.BlockSpec((1, MLA_HEADS, TK, HEAD_PAD), lambda b, i: (b, 0, i, 0)),
                  pl.BlockSpec((1, 1, MLA_HEADS, MLA_V, TK), lambda b, i: (b, i, 0, 0, 0))],
        out_specs=pl.BlockSpec((1, TQ, MLA_HEADS * MLA_V), lambda b, i: (b, i, 0)),
        out_shape=jax.ShapeDtypeStruct((bsz, seq, MLA_HEADS * MLA_V), BF16),
        scratch_shapes=[pltpu.VMEM((MLA_HEADS, 1, TQ), F32), pltpu.VMEM((MLA_HEADS, 1, TQ), F32),
                        pltpu.VMEM((MLA_HEADS, MLA_V, TQ), F32),
                        pltpu.VMEM((seq // TK, MLA_HEADS, TK, HEAD_PAD), BF16),
                        pltpu.VMEM((seq // TK, MLA_HEADS, MLA_V, TK), BF16)],
        compiler_params=_params(2),
        name="attention_running_max" if running_max else "attention",
    )(qt, k, vt)


def _split3(x):
    hi = x.astype(BF16)
    r = x - hi.astype(F32)
    mid = r.astype(BF16)
    lo = (r - mid.astype(F32)).astype(BF16)
    return hi, mid, lo


def _mlstm_kernel(mqt_ref, mk_ref, mvt_ref, mo_ref, g_ref, gt_ref, gmh_ref, yd_ref, ct_ref, m_ref):
    bsz = mqt_ref.shape[0]

    @pl.when(pl.program_id(0) == 0)
    def _():
        ct_ref[...] = jnp.zeros(ct_ref.shape, F32)
        m_ref[...] = jnp.zeros(m_ref.shape, F32)

    row_s = lax.broadcasted_iota(jnp.int32, (ML, ML), 0)
    col_t = lax.broadcasted_iota(jnp.int32, (ML, ML), 1)
    visible = row_s <= col_t
    upper = jnp.where(visible, 1.0, 0.0).astype(BF16)
    lower = jnp.where(col_t <= row_s, 1.0, 0.0).astype(BF16)
    lane = lax.broadcasted_iota(jnp.int32, (ML, LANES), 1)
    k_lo = lane < MLSTM_QK
    ones_rows = jnp.where(lax.broadcasted_iota(jnp.int32, (V_EXT - MLSTM_V, ML), 0) == 0, 1.0, 0.0)
    i_lane0 = MLA_ROPE

    chains = [(b, hd) for b in range(bsz) for hd in range(MLSTM_HEADS)]
    a_row, i_row, c_col = {}, {}, {}
    for b in range(bsz):
        gcol = g_ref[b]
        grow = gt_ref[b]
        f_al = pltpu.roll(gcol, LANES - MLSTM_HEADS, axis=1)
        a_al = sum(_dot(lower, piece) for piece in _split3(f_al))
        a_rows = sum(_dot(piece, upper) for piece in _split3(jnp.concatenate([grow, grow], axis=0)))
        c_all = gcol - a_al
        for hd in range(MLSTM_HEADS):
            a_row[b, hd] = a_rows[MLSTM_HEADS + hd:MLSTM_HEADS + hd + 1]
            i_row[b, hd] = grow[hd:hd + 1]
            c_col[b, hd] = c_all[:, i_lane0 + hd:i_lane0 + hd + 1]

    k_m, qt_pair, vt_ext, sqk, qc = {}, {}, {}, {}, {}
    for b, hd in chains:
        pair, half = hd // 2, hd % 2
        k_pair = mk_ref[b, :, pair * LANES:(pair + 1) * LANES]
        k_m[b, hd] = jnp.where(k_lo if half == 0 else ~k_lo, k_pair.astype(F32), 0.0).astype(BF16)
        qt_pair[b, hd] = mqt_ref[b, pair * LANES:(pair + 1) * LANES, :]
        sqk[b, hd] = _dot(k_m[b, hd], qt_pair[b, hd])
    for b, hd in chains:
        idx = b * MLSTM_HEADS + hd
        qc[b, hd] = _dot(ct_ref[idx].astype(BF16), qt_pair[b, hd])

    m_t, m_new, m_prev, w_inter, s_t, den_intra = {}, {}, {}, {}, {}, {}
    for b, hd in chains:
        idx = b * MLSTM_HEADS + hd
        m_prev[b, hd] = m_ref[idx][0:1, 0:1]
        d_t = jnp.where(visible, a_row[b, hd] + c_col[b, hd], -jnp.inf)
        inter = a_row[b, hd] + m_prev[b, hd]
        m_t[b, hd] = jnp.maximum(inter, jnp.max(d_t, axis=0, keepdims=True))
        w_inter[b, hd] = jnp.exp(inter - m_t[b, hd])
        s_t[b, hd] = sqk[b, hd] * jnp.exp(d_t - m_t[b, hd])
        den_intra[b, hd] = jnp.sum(s_t[b, hd], axis=0, keepdims=True)

    nv = {}
    for b, hd in chains:
        v_t = mvt_ref[b, hd * MLSTM_V:(hd + 1) * MLSTM_V, :]
        vt_ext[b, hd] = jnp.concatenate([v_t.astype(F32), ones_rows], axis=0)
        nv[b, hd] = _dot(vt_ext[b, hd].astype(BF16), s_t[b, hd].astype(BF16))

    for b, hd in chains:
        idx = b * MLSTM_HEADS + hd
        a_last = a_row[b, hd][:, ML - 1:ML]
        m_new[b, hd] = m_t[b, hd][:, ML - 1:ML]
        w_s = jnp.exp(a_last - a_row[b, hd] + i_row[b, hd] - m_new[b, hd])
        decay = jnp.exp(a_last + m_prev[b, hd] - m_new[b, hd])
        upd = _dot((vt_ext[b, hd] * w_s).astype(BF16), k_m[b, hd])
        ct_ref[idx] = decay * ct_ref[idx] + upd
        m_ref[idx] = jnp.broadcast_to(m_new[b, hd], (8, LANES))

    for b in range(bsz):
        h_t = []
        for hd in range(MLSTM_HEADS):
            num = nv[b, hd][:MLSTM_V] + w_inter[b, hd] * qc[b, hd][:MLSTM_V]
            den = den_intra[b, hd] + w_inter[b, hd] * qc[b, hd][MLSTM_V:MLSTM_V + 1]
            hh = num / jnp.maximum(jnp.abs(den), jnp.exp(-m_t[b, hd]))
            h_t.append(hh * lax.rsqrt(jnp.mean(hh * hh, axis=0, keepdims=True) + EPS))
        hn = jnp.concatenate(h_t, axis=0).T * gmh_ref[...]
        yd_ref[b] = (jax.nn.sigmoid(mo_ref[b]) * hn).astype(BF16)


def _mlstm(mqt, mk, mvt, mo, g, gt, g_mh):
    bsz, seq, _ = mk.shape
    tile3 = lambda w: pl.BlockSpec((bsz, ML, w), lambda i: (0, i, 0))
    tile3t = lambda r: pl.BlockSpec((bsz, r, ML), lambda i: (0, 0, i))
    return pl.pallas_call(
        _mlstm_kernel,
        grid=(seq // ML,),
        in_specs=[tile3t(256), tile3(256), tile3t(512), tile3(512),
                  tile3(LANES), tile3t(2 * MLSTM_HEADS), _const_spec((1, MLSTM_HEADS * MLSTM_V))],
        out_specs=tile3(512),
        out_shape=jax.ShapeDtypeStruct((bsz, seq, MLSTM_HEADS * MLSTM_V), BF16),
        scratch_shapes=[pltpu.VMEM((bsz * MLSTM_HEADS, V_EXT, 2 * MLSTM_QK), F32),
                        pltpu.VMEM((bsz * MLSTM_HEADS, 8, LANES), F32)],
        compiler_params=_params(1),
        name="mlstm",
    )(mqt, mk, mvt, mo, g, gt, g_mh.reshape(1, MLSTM_HEADS * MLSTM_V))


def kernel(x, p, positions, g_mix, g_ffn, g_ple, ev_w_in, ev_w_conv, ev_g_v, ev_w_s, ev_b_s, ev_w_out,
           od_w_in, od_b_gate, od_g_qa, od_g_kva, od_w_q_up, od_w_kv_up, od_g_q, od_g_k, od_g_mh, od_w_out,
           w_gate, w_up, w_down, w_ple_proj, w_ple_gate):
    h = x
    depth = g_mix.shape[0]
    ffn_w = tuple(w.astype(BF16) for w in (w_gate, w_up, w_down, w_ple_gate, w_ple_proj))
    for layer in range(depth):
        j = layer // 2
        if layer % 2 == 0:
            y1, y2 = _even_mixer(h, g_mix[layer], ev_w_in[j], ev_w_conv[j], ev_g_v[j], ev_w_s[j], ev_b_s[j])
            w_out = ev_w_out[j]
        else:
            qt, k, vt, spread, mqt, mk, mvt, mo, g, gt = _odd_proj(
                h, positions, g_mix[layer], od_w_in[j], od_b_gate[j], od_g_qa[j], od_g_kva[j],
                od_w_q_up[j], od_w_kv_up[j], od_g_q[j], od_g_k[j])
            y1 = lax.cond(jnp.max(spread) <= MAX_SCORE_SPREAD,
                          functools.partial(_attention, running_max=False),
                          functools.partial(_attention, running_max=True), qt, k, vt)
            y2 = _mlstm(mqt, mk, mvt, mo, g, gt, od_g_mh[j])
            w_out = od_w_out[j]
        h = _ffn_ple(layer, h, y1, y2, p, w_out, g_ffn[layer], g_ple[layer], *ffn_w)
    return h
```

```python
import functools
import math

import numpy as np
import jax
import jax.numpy as jnp
from jax import lax
from jax.experimental import pallas as pl
from jax.experimental.pallas import tpu as pltpu

F32 = jnp.float32
BF16 = jnp.bfloat16

D_MODEL = 1024
D_PLE = 256
EPS = 1e-6

CONV_DIM = 512
GMLP_DIM = 512
GMLP_HEADS = 8
GMLP_HEAD_DIM = 64
GMLP_CHUNK = 128
EVEN_IN = 3 * CONV_DIM + 2 * GMLP_DIM

MLA_HEADS = 8
MLA_NOPE = 64
MLA_ROPE = 32
MLA_V = 64
MLA_QK = MLA_NOPE + MLA_ROPE
Q_LORA = 384
KV_LORA = 256
ROPE_THETA = 10000.0
MLSTM_HEADS = 4
MLSTM_QK = 64
MLSTM_V = 128
D_FF = 2816

LANES = 128
MXU_EDGE = 256

TM = 512
TM_FFN = 512
TQ = 512
TK = 512
KSUB = 256
LOOKAHEAD = 2
ML = 256
V_EXT = MLSTM_V + 16
HEAD_PAD = 128
ODD_IN_PAD = 2304
TAIL_OFF = ODD_IN_PAD - LANES
FF_CHUNKS = ((0, 1024), (1024, 2048), (2048, 2816))
MAX_SCORE_SPREAD = 200.0
SCORE_SCALE = MLA_QK ** -0.5 * math.log2(math.e)
BF16_SLACK = 1.01

VMEM_LIMIT = 56 * 1024 * 1024


def _dot(a, b):
    return jnp.dot(a, b, preferred_element_type=F32)


def _dot_nt(a, b):
    return lax.dot_general(a, b, (((1,), (1,)), ((), ())), preferred_element_type=F32)


def _rms_rows(x, g):
    ms = jnp.mean(x * x, axis=-1, keepdims=True)
    return x * lax.rsqrt(ms + EPS) * g


def _rms_cols(x, g):
    ms = jnp.mean(x * x, axis=0, keepdims=True)
    return x * lax.rsqrt(ms + EPS) * g


def _sigmoid(x):
    return 0.5 * jnp.tanh(0.5 * x) + 0.5


def _const_spec(shape):
    zeros = (0,) * len(shape)
    return pl.BlockSpec(shape, lambda *_: zeros, pipeline_mode=pl.Buffered(1))


def _params(n_axes):
    return pltpu.CompilerParams(dimension_semantics=("arbitrary",) * n_axes,
                                vmem_limit_bytes=VMEM_LIMIT)


def _even_kernel(h_ref, gmix_ref, win_ref, wconv_ref, gmat_ref, gv_ref, wpair_ref, bm_ref,
                 ya_ref, yb_ref, zbuf_ref):
    @pl.when(pl.program_id(1) == 0)
    def _():
        zbuf_ref[0:8, :] = jnp.zeros((8, CONV_DIM), F32)

    x = h_ref[0]
    hn = _rms_rows(x, gmix_ref[...]).astype(BF16)
    z_uv = _dot(hn, win_ref[:, 3 * CONV_DIM:EVEN_IN])
    u = z_uv[:, 0:GMLP_DIM]
    v = z_uv[:, GMLP_DIM:2 * GMLP_DIM]
    z_conv = _dot(hn, win_ref[:, 0:3 * CONV_DIM])
    b_gate = z_conv[:, 0:512]
    c_gate = z_conv[:, 512:1024]
    x_in = z_conv[:, 1024:1536]

    zz = c_gate * x_in
    zbuf_ref[8:8 + TM, :] = zz
    z1 = zbuf_ref[7:7 + TM, :]
    z2 = zbuf_ref[6:6 + TM, :]
    wc = wconv_ref[...]
    conv = wc[2:3] * zz + wc[1:2] * z1 + wc[0:1] * z2
    ya_ref[0] = (b_gate * conv).astype(BF16)
    zbuf_ref[0:8, :] = zz[TM - 8:TM, :]

    gu = jax.nn.gelu(u)
    gv = jax.nn.gelu(v)
    ss = _dot((gv * gv).astype(BF16), gmat_ref[...])
    vn = gv * lax.rsqrt(ss * (1.0 / GMLP_HEAD_DIM) + EPS) * gv_ref[...]
    lane = lax.broadcasted_iota(jnp.int32, (GMLP_CHUNK, LANES), 1)
    row_t = lax.broadcasted_iota(jnp.int32, (GMLP_CHUNK, 2 * GMLP_CHUNK), 0)
    col_s = lax.broadcasted_iota(jnp.int32, (GMLP_CHUNK, 2 * GMLP_CHUNK), 1) % GMLP_CHUNK
    tril = col_s <= row_t
    wms = [jnp.where(tril, wpair_ref[j], 0.0).astype(BF16) for j in range(GMLP_HEADS // 2)]
    for c in range(TM // GMLP_CHUNK):
        r0 = c * GMLP_CHUNK
        outs = []
        for j in range(GMLP_HEADS // 2):
            vp = vn[r0:r0 + GMLP_CHUNK, j * LANES:(j + 1) * LANES]
            lo = jnp.where(lane < GMLP_HEAD_DIM, vp, 0.0).astype(BF16)
            hi = jnp.where(lane >= GMLP_HEAD_DIM, vp, 0.0).astype(BF16)
            outs.append(_dot(wms[j], jnp.concatenate([lo, hi], axis=0)))
        mixed = jnp.concatenate(outs, axis=1) + bm_ref[...]
        yb_ref[0, r0:r0 + GMLP_CHUNK, :] = (gu[r0:r0 + GMLP_CHUNK] * mixed).astype(BF16)


def _even_mixer(h, g_mix, w_in, w_conv, g_v, w_s, b_s):
    bsz, seq, _ = h.shape
    gmat = jnp.asarray(np.kron(np.eye(GMLP_HEADS), np.ones((GMLP_HEAD_DIM, GMLP_HEAD_DIM))), BF16)
    wpair = w_s.reshape(GMLP_HEADS // 2, 2, GMLP_CHUNK, GMLP_CHUNK).transpose(0, 2, 1, 3)
    wpair = wpair.reshape(GMLP_HEADS // 2, GMLP_CHUNK, 2 * GMLP_CHUNK)
    bm = jnp.repeat(b_s.T, GMLP_HEAD_DIM, axis=1)
    out_sds = jax.ShapeDtypeStruct((bsz, seq, CONV_DIM), BF16)
    tile = lambda w: pl.BlockSpec((1, TM, w), lambda b, i: (b, i, 0))
    return pl.pallas_call(
        _even_kernel,
        grid=(bsz, seq // TM),
        in_specs=[tile(D_MODEL), _const_spec((1, D_MODEL)), _const_spec((D_MODEL, EVEN_IN)),
                  _const_spec((3, CONV_DIM)), _const_spec((GMLP_DIM, GMLP_DIM)), _const_spec((1, GMLP_DIM)),
                  _const_spec((GMLP_HEADS // 2, GMLP_CHUNK, 2 * GMLP_CHUNK)),
                  _const_spec((GMLP_CHUNK, GMLP_DIM))],
        out_specs=[tile(CONV_DIM), tile(GMLP_DIM)],
        out_shape=[out_sds, out_sds],
        scratch_shapes=[pltpu.VMEM((TM + 8, CONV_DIM), F32)],
        compiler_params=_params(2),
        name="even_mixer",
    )(h, g_mix.reshape(1, D_MODEL), w_in.astype(BF16), w_conv, gmat, g_v.reshape(1, GMLP_DIM), wpair, bm)


def _ffn_kernel(h_ref, y1_ref, y2_ref, p_ref, wout_ref, gffn_ref, wg_ref, wu_ref, wd_ref,
                gple_ref, wpg_ref, wpp_ref, o_ref):
    rows = [slice(r, r + TM_FFN // 2) for r in (0, TM_FFN // 2)]
    p_b = p_ref[0, 0].astype(BF16)
    half = D_MODEL // 2
    xs = [h_ref[0, r] + _dot(jnp.concatenate([y1_ref[0, r], y2_ref[0, r]], axis=-1), wout_ref[...])
          for r in rows]
    pp_lo = _dot(p_b, wpp_ref[:, :half])
    x = jnp.concatenate(xs, axis=0)
    hn = jnp.concatenate([_rms_rows(xr, gffn_ref[...]).astype(BF16) for xr in xs], axis=0)
    acc = x
    for lo, hi in FF_CHUNKS:
        g = _dot(hn, wg_ref[:, lo:hi])
        u = _dot(hn, wu_ref[:, lo:hi])
        a = (g * _sigmoid(g) * u).astype(BF16)
        acc = acc + _dot(a, wd_ref[lo:hi, :])
    pp_hi = _dot(p_b, wpp_ref[:, half:])
    pp = jnp.concatenate([pp_lo, pp_hi], axis=-1)
    for r in rows:
        hn2 = _rms_rows(acc[r], gple_ref[...]).astype(BF16)
        o_ref[0, r] = acc[r] + _sigmoid(_dot(hn2, wpg_ref[...])) * pp[r]


def _ffn_ple(layer, h, y1, y2, p, w_out, g_ffn, g_ple, w_gate, w_up, w_down, w_pg, w_pp):
    bsz, seq, _ = h.shape
    tile = lambda w: pl.BlockSpec((1, TM_FFN, w), lambda b, i: (b, i, 0))
    layer_spec = lambda r, c: pl.BlockSpec((None, r, c), lambda *_: (layer, 0, 0), pipeline_mode=pl.Buffered(1))
    return pl.pallas_call(
        _ffn_kernel,
        grid=(bsz, seq // TM_FFN),
        in_specs=[tile(D_MODEL), tile(512), tile(512),
                  pl.BlockSpec((1, 1, TM_FFN, D_PLE), lambda b, i: (layer, b, i, 0)),
                  _const_spec((D_MODEL, D_MODEL)), _const_spec((1, D_MODEL)),
                  layer_spec(D_MODEL, D_FF), layer_spec(D_MODEL, D_FF), layer_spec(D_FF, D_MODEL),
                  _const_spec((1, D_MODEL)), layer_spec(D_MODEL, D_MODEL), layer_spec(D_PLE, D_MODEL)],
        out_specs=tile(D_MODEL),
        out_shape=jax.ShapeDtypeStruct((bsz, seq, D_MODEL), F32),
        compiler_params=_params(2),
        name="ffn_ple",
    )(h, y1, y2, p, w_out.astype(BF16), g_ffn.reshape(1, D_MODEL), w_gate, w_up, w_down,
      g_ple.reshape(1, D_MODEL), w_pg, w_pp)


def _log_sigmoid(x):
    return jnp.minimum(x, 0.0) - jnp.log1p(jnp.exp(-jnp.abs(x)))


def _rope_cols(xr, cos, sin):
    half = MLA_ROPE // 2
    x1, x2 = xr[:half], xr[half:]
    return x1 * cos - x2 * sin, x2 * cos + x1 * sin


def _odd_kernel(h_ref, pos_ref, gmix_ref, win_ref, bias_ref, gqa_ref, gkva_ref, wqt_ref, wkvt_ref,
                gqn_ref, gqr_ref, gkn_ref, gkr_ref, freq_ref,
                qt_ref, k_ref, vt_ref, spread_ref, mqt_ref, mk_ref, mvt_ref, mo_ref, g_ref, gt_ref):
    x = h_ref[0]
    hn = _rms_rows(x, gmix_ref[...]).astype(BF16)
    z = _dot(hn, win_ref[...])
    q_lat = z[:, 0:384]
    kv_lat = z[:, 384:640]

    mqt_ref[0] = (z[:, 640:896] * (MLSTM_QK ** -0.5)).T.astype(BF16)
    mk_ref[0] = z[:, 896:1152].astype(BF16)
    mvt_ref[0] = z[:, 1152:1664].T.astype(BF16)
    mo_ref[0] = z[:, 1664:2176]
    tail = z[:, TAIL_OFF:ODD_IN_PAD] + bias_ref[...]
    lane = lax.broadcasted_iota(jnp.int32, (TM, LANES), 1)
    f_lane = (lane >= MLA_ROPE + MLSTM_HEADS) & (lane < MLA_ROPE + 2 * MLSTM_HEADS)
    tail = jnp.where(f_lane, _log_sigmoid(tail), tail)
    g_ref[0] = tail
    tail_t = tail.T
    gt_ref[0] = tail_t[MLA_ROPE:MLA_ROPE + 2 * MLSTM_HEADS]

    qn = _rms_rows(q_lat, gqa_ref[...]).astype(BF16)
    kvn = _rms_rows(kv_lat, gkva_ref[...]).astype(BF16)
    q_t = _dot_nt(wqt_ref[...], qn)
    kv_t = _dot_nt(wkvt_ref[...], kvn)

    ang = freq_ref[...] * pos_ref[0].astype(F32)
    cos = jnp.cos(ang)
    sin = jnp.sin(ang)
    kr1, kr2 = _rope_cols(_rms_cols(tail_t[0:MLA_ROPE], gkr_ref[...]), cos, sin)

    norm_bound = lambda gn_ref, gr_ref: jnp.sqrt(
        MLA_NOPE * jnp.max(gn_ref[...] ** 2, axis=0, keepdims=True)
        + MLA_ROPE * jnp.max(gr_ref[...] ** 2, axis=0, keepdims=True))
    upper = BF16_SLACK * norm_bound(gqn_ref, gqr_ref) * norm_bound(gkn_ref, gkr_ref)
    first_row = lax.broadcasted_iota(jnp.int32, (8, TM), 0) == 0
    zeros_tail = jnp.zeros((HEAD_PAD - MLA_QK - 8, TM), F32)
    k_pad = jnp.concatenate([jnp.ones((8, TM), F32), zeros_tail], axis=0)
    spreads = []
    for hd in range(MLA_HEADS):
        q0 = hd * MLA_QK
        q_nope = _rms_cols(q_t[q0:q0 + MLA_NOPE], gqn_ref[...])
        qr1, qr2 = _rope_cols(_rms_cols(q_t[q0 + MLA_NOPE:q0 + MLA_QK], gqr_ref[...]), cos, sin)
        q96 = jnp.concatenate([q_nope, qr1, qr2], axis=0)
        k0 = hd * (MLA_NOPE + MLA_V)
        k_nope = _rms_cols(kv_t[k0:k0 + MLA_NOPE], gkn_ref[...])
        k96 = jnp.concatenate([k_nope, kr1, kr2], axis=0)
        lower = jnp.sum(q96 * k96, axis=0, keepdims=True)
        shift = 0.5 * (upper + lower)
        spreads.append(upper - lower)
        q_full = jnp.concatenate([q96, jnp.where(first_row, -shift, 0.0), zeros_tail], axis=0)
        qt_ref[0, hd] = q_full.astype(BF16)
        k_ref[0, hd] = jnp.concatenate([k96, k_pad], axis=0).T.astype(BF16)
        vt_ref[0, 0, hd] = kv_t[k0 + MLA_NOPE:k0 + MLA_NOPE + MLA_V].astype(BF16)
    spread_ref[0] = jnp.concatenate(spreads, axis=0)


def _odd_proj(h, positions, g_mix, w_in, b_gate, g_qa, g_kva, w_q_up, w_kv_up, g_q, g_k):
    bsz, seq, _ = h.shape
    c = np.cumsum([0, Q_LORA + KV_LORA, MLA_ROPE, 2 * MLSTM_HEADS * (MLSTM_QK + MLSTM_V), 2 * MLSTM_HEADS])
    seg = lambda i: w_in[:, c[i]:c[i + 1]].astype(BF16)
    w_in_r = jnp.zeros((D_MODEL, ODD_IN_PAD), BF16)
    for col0, i in ((0, 0), (c[1], 2), (TAIL_OFF, 1), (TAIL_OFF + MLA_ROPE, 3)):
        w_in_r = lax.dynamic_update_slice(w_in_r, seg(i), (0, col0))
    bias = jnp.zeros((1, LANES), F32).at[0, MLA_ROPE:MLA_ROPE + 2 * MLSTM_HEADS].set(b_gate)
    inv_freq = ROPE_THETA ** (-jnp.arange(0, MLA_ROPE, 2, dtype=F32) / MLA_ROPE)
    col = lambda a: a.reshape(-1, 1)
    tile3 = lambda w: pl.BlockSpec((1, TM, w), lambda b, i: (b, i, 0))
    out_shape = [
        jax.ShapeDtypeStruct((bsz, MLA_HEADS, HEAD_PAD, seq), BF16),
        jax.ShapeDtypeStruct((bsz, MLA_HEADS, seq, HEAD_PAD), BF16),
        jax.ShapeDtypeStruct((bsz, seq // TK, MLA_HEADS, MLA_V, TK), BF16),
        jax.ShapeDtypeStruct((bsz, MLA_HEADS, seq), F32),
        jax.ShapeDtypeStruct((bsz, 256, seq), BF16),
        jax.ShapeDtypeStruct((bsz, seq, 256), BF16),
        jax.ShapeDtypeStruct((bsz, 512, seq), BF16),
        jax.ShapeDtypeStruct((bsz, seq, 512), F32),
        jax.ShapeDtypeStruct((bsz, seq, LANES), F32),
        jax.ShapeDtypeStruct((bsz, 2 * MLSTM_HEADS, seq), F32),
    ]
    out_specs = [
        pl.BlockSpec((1, MLA_HEADS, HEAD_PAD, TM), lambda b, i: (b, 0, 0, i)),
        pl.BlockSpec((1, MLA_HEADS, TM, HEAD_PAD), lambda b, i: (b, 0, i, 0)),
        pl.BlockSpec((1, 1, MLA_HEADS, MLA_V, TK), lambda b, i: (b, i, 0, 0, 0)),
        pl.BlockSpec((1, MLA_HEADS, TM), lambda b, i: (b, 0, i)),
        pl.BlockSpec((1, 256, TM), lambda b, i: (b, 0, i)),
        tile3(256),
        pl.BlockSpec((1, 512, TM), lambda b, i: (b, 0, i)),
        tile3(512), tile3(LANES),
        pl.BlockSpec((1, 2 * MLSTM_HEADS, TM), lambda b, i: (b, 0, i)),
    ]
    return pl.pallas_call(
        _odd_kernel,
        grid=(bsz, seq // TM),
        in_specs=[tile3(D_MODEL), pl.BlockSpec((1, 1, TM), lambda b, i: (b, 0, i)),
                  _const_spec((1, D_MODEL)), _const_spec((D_MODEL, ODD_IN_PAD)), _const_spec((1, LANES)),
                  _const_spec((1, Q_LORA)), _const_spec((1, KV_LORA)),
                  _const_spec((MLA_HEADS * MLA_QK, Q_LORA)), _const_spec((MLA_HEADS * (MLA_NOPE + MLA_V), KV_LORA)),
                  _const_spec((MLA_NOPE, 1)), _const_spec((MLA_ROPE, 1)),
                  _const_spec((MLA_NOPE, 1)), _const_spec((MLA_ROPE, 1)), _const_spec((MLA_ROPE // 2, 1))],
        out_specs=out_specs,
        out_shape=out_shape,
        compiler_params=_params(2),
        name="odd_proj",
    )(h, positions.reshape(bsz, 1, seq), g_mix.reshape(1, D_MODEL), w_in_r, bias,
      g_qa.reshape(1, Q_LORA), g_kva.reshape(1, KV_LORA), w_q_up.T.astype(BF16), w_kv_up.T.astype(BF16),
      col(g_q[:MLA_NOPE] * SCORE_SCALE), col(g_q[MLA_NOPE:] * SCORE_SCALE),
      col(g_k[:MLA_NOPE]), col(g_k[MLA_NOPE:]),
      inv_freq.reshape(-1, 1))


def _attn_kernel(qt_ref, k_ref, vt_ref, o_ref, m_ref, l_ref, acc_ref, kbuf_ref, vbuf_ref, *, running_max):
    qi = pl.program_id(1)
    m_ref[...] = jnp.full(m_ref.shape, -jnp.inf, F32)
    l_ref[...] = jnp.zeros(l_ref.shape, F32)
    acc_ref[...] = jnp.zeros(acc_ref.shape, F32)
    kbuf_ref[qi] = k_ref[0]
    vbuf_ref[qi] = vt_ref[0, 0]

    units = [(hd, part) for hd in range(MLA_HEADS) for part in range(TK // KSUB)]

    def scores(j, unit, t0):
        hd, part = unit
        k = kbuf_ref[j, hd, part * KSUB:(part + 1) * KSUB, :]
        return _dot(k, qt_ref[0, hd, :, t0:])

    def kv_tile(j, masked):
        first_q = lambda unit: unit[1] * KSUB if masked else 0
        pending = [scores(j, u, first_q(u)) for u in units[:LOOKAHEAD]]
        for i, (hd, part) in enumerate(units):
            s = pending.pop(0)
            if i + LOOKAHEAD < len(units):
                nxt = units[i + LOOKAHEAD]
                pending.append(scores(j, nxt, first_q(nxt)))
            t0 = first_q((hd, part))
            if masked:
                row_s = lax.broadcasted_iota(jnp.int32, s.shape, 0) + part * KSUB
                col_t = lax.broadcasted_iota(jnp.int32, s.shape, 1) + t0
                s = jnp.where(row_s <= col_t, s, -jnp.inf)
            v_t = vbuf_ref[j, hd, :, part * KSUB:(part + 1) * KSUB]
            if running_max:
                m_old = m_ref[hd, :, t0:]
                m_new = jnp.maximum(m_old, jnp.max(s, axis=0, keepdims=True))
                alpha = jnp.exp2(m_old - m_new)
                p = jnp.exp2(s - m_new)
                m_ref[hd, :, t0:] = m_new
                l_ref[hd, :, t0:] = alpha * l_ref[hd, :, t0:] + jnp.sum(p, axis=0, keepdims=True)
                acc_ref[hd, :, t0:] = alpha * acc_ref[hd, :, t0:] + _dot(v_t, p.astype(BF16))
            else:
                p = jnp.exp2(s)
                l_ref[hd, :, t0:] += jnp.sum(p, axis=0, keepdims=True)
                acc_ref[hd, :, t0:] += _dot(v_t, p.astype(BF16))

    def body(j, _):
        kv_tile(j, False)
        return 0

    lax.fori_loop(0, qi, body, 0)
    kv_tile(qi, True)
    out_t = jnp.concatenate([acc_ref[hd] / l_ref[hd] for hd in range(MLA_HEADS)], axis=0)
    o_ref[0] = out_t.T.astype(BF16)


def _attention(qt, k, vt, *, running_max):
    bsz, _, _, seq = qt.shape
    return pl.pallas_call(
        functools.partial(_attn_kernel, running_max=running_max),
        grid=(bsz, seq // TQ),
        in_specs=[pl.BlockSpec((1, MLA_HEADS, HEAD_PAD, TQ), lambda b, i: (b, 0, 0, i)),
                  pl.BlockSpec((1, MLA_HEADS, TK, HEAD_PAD), lambda b, i: (b, 0, i, 0)),
                  pl.BlockSpec((1, 1, MLA_HEADS, MLA_V, TK), lambda b, i: (b, i, 0, 0, 0))],
        out_specs=pl.BlockSpec((1, TQ, MLA_HEADS * MLA_V), lambda b, i: (b, i, 0)),
        out_shape=jax.ShapeDtypeStruct((bsz, seq, MLA_HEADS * MLA_V), BF16),
        scratch_shapes=[pltpu.VMEM((MLA_HEADS, 1, TQ), F32), pltpu.VMEM((MLA_HEADS, 1, TQ), F32),
                        pltpu.VMEM((MLA_HEADS, MLA_V, TQ), F32),
                        pltpu.VMEM((seq // TK, MLA_HEADS, TK, HEAD_PAD), BF16),
                        pltpu.VMEM((seq // TK, MLA_HEADS, MLA_V, TK), BF16)],
        compiler_params=_params(2),
        name="attention_running_max" if running_max else "attention",
    )(qt, k, vt)


def _split3(x):
    hi = x.astype(BF16)
    r = x - hi.astype(F32)
    mid = r.astype(BF16)
    lo = (r - mid.astype(F32)).astype(BF16)
    return hi, mid, lo


def _mlstm_kernel(mqt_ref, mk_ref, mvt_ref, mo_ref, g_ref, gt_ref, gmh_ref, yd_ref, ct_ref, m_ref):
    bsz = mqt_ref.shape[0]

    @pl.when(pl.program_id(0) == 0)
    def _():
        ct_ref[...] = jnp.zeros(ct_ref.shape, F32)
        m_ref[...] = jnp.zeros(m_ref.shape, F32)

    row_s = lax.broadcasted_iota(jnp.int32, (ML, ML), 0)
    col_t = lax.broadcasted_iota(jnp.int32, (ML, ML), 1)
    visible = row_s <= col_t
    upper = jnp.where(visible, 1.0, 0.0).astype(BF16)
    lower = jnp.where(col_t <= row_s, 1.0, 0.0).astype(BF16)
    lane = lax.broadcasted_iota(jnp.int32, (ML, LANES), 1)
    k_lo = lane < MLSTM_QK
    ones_rows = jnp.where(lax.broadcasted_iota(jnp.int32, (V_EXT - MLSTM_V, ML), 0) == 0, 1.0, 0.0)
    i_lane0 = MLA_ROPE

    chains = [(b, hd) for b in range(bsz) for hd in range(MLSTM_HEADS)]
    a_row, i_row, c_col = {}, {}, {}
    for b in range(bsz):
        gcol = g_ref[b]
        grow = gt_ref[b]
        f_al = pltpu.roll(gcol, LANES - MLSTM_HEADS, axis=1)
        a_al = sum(_dot(lower, piece) for piece in _split3(f_al))
        a_rows = sum(_dot(piece, upper) for piece in _split3(jnp.concatenate([grow, grow], axis=0)))
        c_all = gcol - a_al
        for hd in range(MLSTM_HEADS):
            a_row[b, hd] = a_rows[MLSTM_HEADS + hd:MLSTM_HEADS + hd + 1]
            i_row[b, hd] = grow[hd:hd + 1]
            c_col[b, hd] = c_all[:, i_lane0 + hd:i_lane0 + hd + 1]

    k_m, qt_pair, vt_ext, sqk, qc = {}, {}, {}, {}, {}
    for b, hd in chains:
        pair, half = hd // 2, hd % 2
        k_pair = mk_ref[b, :, pair * LANES:(pair + 1) * LANES]
        k_m[b, hd] = jnp.where(k_lo if half == 0 else ~k_lo, k_pair.astype(F32), 0.0).astype(BF16)
        qt_pair[b, hd] = mqt_ref[b, pair * LANES:(pair + 1) * LANES, :]
        sqk[b, hd] = _dot(k_m[b, hd], qt_pair[b, hd])
    for b, hd in chains:
        idx = b * MLSTM_HEADS + hd
        qc[b, hd] = _dot(ct_ref[idx].astype(BF16), qt_pair[b, hd])

    m_t, m_new, m_prev, w_inter, s_t, den_intra = {}, {}, {}, {}, {}, {}
    for b, hd in chains:
        idx = b * MLSTM_HEADS + hd
        m_prev[b, hd] = m_ref[idx][0:1, 0:1]
        d_t = jnp.where(visible, a_row[b, hd] + c_col[b, hd], -jnp.inf)
        inter = a_row[b, hd] + m_prev[b, hd]
        m_t[b, hd] = jnp.maximum(inter, jnp.max(d_t, axis=0, keepdims=True))
        w_inter[b, hd] = jnp.exp(inter - m_t[b, hd])
        s_t[b, hd] = sqk[b, hd] * jnp.exp(d_t - m_t[b, hd])
        den_intra[b, hd] = jnp.sum(s_t[b, hd], axis=0, keepdims=True)

    nv = {}
    for b, hd in chains:
        v_t = mvt_ref[b, hd * MLSTM_V:(hd + 1) * MLSTM_V, :]
        vt_ext[b, hd] = jnp.concatenate([v_t.astype(F32), ones_rows], axis=0)
        nv[b, hd] = _dot(vt_ext[b, hd].astype(BF16), s_t[b, hd].astype(BF16))

    for b, hd in chains:
        idx = b * MLSTM_HEADS + hd
        a_last = a_row[b, hd][:, ML - 1:ML]
        m_new[b, hd] = m_t[b, hd][:, ML - 1:ML]
        w_s = jnp.exp(a_last - a_row[b, hd] + i_row[b, hd] - m_new[b, hd])
        decay = jnp.exp(a_last + m_prev[b, hd] - m_new[b, hd])
        upd = _dot((vt_ext[b, hd] * w_s).astype(BF16), k_m[b, hd])
        ct_ref[idx] = decay * ct_ref[idx] + upd
        m_ref[idx] = jnp.broadcast_to(m_new[b, hd], (8, LANES))

    for b in range(bsz):
        h_t = []
        for hd in range(MLSTM_HEADS):
            num = nv[b, hd][:MLSTM_V] + w_inter[b, hd] * qc[b, hd][:MLSTM_V]
            den = den_intra[b, hd] + w_inter[b, hd] * qc[b, hd][MLSTM_V:MLSTM_V + 1]
            hh = num / jnp.maximum(jnp.abs(den), jnp.exp(-m_t[b, hd]))
            h_t.append(hh * lax.rsqrt(jnp.mean(hh * hh, axis=0, keepdims=True) + EPS))
        hn = jnp.concatenate(h_t, axis=0).T * gmh_ref[...]
        yd_ref[b] = (_sigmoid(mo_ref[b]) * hn).astype(BF16)


def _mlstm(mqt, mk, mvt, mo, g, gt, g_mh):
    bsz, seq, _ = mk.shape
    tile3 = lambda w: pl.BlockSpec((bsz, ML, w), lambda i: (0, i, 0))
    tile3t = lambda r: pl.BlockSpec((bsz, r, ML), lambda i: (0, 0, i))
    return pl.pallas_call(
        _mlstm_kernel,
        grid=(seq // ML,),
        in_specs=[tile3t(256), tile3(256), tile3t(512), tile3(512),
                  tile3(LANES), tile3t(2 * MLSTM_HEADS), _const_spec((1, MLSTM_HEADS * MLSTM_V))],
        out_specs=tile3(512),
        out_shape=jax.ShapeDtypeStruct((bsz, seq, MLSTM_HEADS * MLSTM_V), BF16),
        scratch_shapes=[pltpu.VMEM((bsz * MLSTM_HEADS, V_EXT, 2 * MLSTM_QK), F32),
                        pltpu.VMEM((bsz * MLSTM_HEADS, 8, LANES), F32)],
        compiler_params=_params(1),
        name="mlstm",
    )(mqt, mk, mvt, mo, g, gt, g_mh.reshape(1, MLSTM_HEADS * MLSTM_V))


def kernel(x, p, positions, g_mix, g_ffn, g_ple, ev_w_in, ev_w_conv, ev_g_v, ev_w_s, ev_b_s, ev_w_out,
           od_w_in, od_b_gate, od_g_qa, od_g_kva, od_w_q_up, od_w_kv_up, od_g_q, od_g_k, od_g_mh, od_w_out,
           w_gate, w_up, w_down, w_ple_proj, w_ple_gate):
    h = x
    depth = g_mix.shape[0]
    ffn_w = tuple(w.astype(BF16) for w in (w_gate, w_up, w_down, w_ple_gate, w_ple_proj))
    for layer in range(depth):
        j = layer // 2
        if layer % 2 == 0:
            y1, y2 = _even_mixer(h, g_mix[layer], ev_w_in[j], ev_w_conv[j], ev_g_v[j], ev_w_s[j], ev_b_s[j])
            w_out = ev_w_out[j]
        else:
            qt, k, vt, spread, mqt, mk, mvt, mo, g, gt = _odd_proj(
                h, positions, g_mix[layer], od_w_in[j], od_b_gate[j], od_g_qa[j], od_g_kva[j],
                od_w_q_up[j], od_w_kv_up[j], od_g_q[j], od_g_k[j])
            y1 = lax.cond(jnp.max(spread) <= MAX_SCORE_SPREAD,
                          functools.partial(_attention, running_max=False),
                          functools.partial(_attention, running_max=True), qt, k, vt)
            y2 = _mlstm(mqt, mk, mvt, mo, g, gt, od_g_mh[j])
            w_out = od_w_out[j]
        h = _ffn_ple(layer, h, y1, y2, p, w_out, g_ffn[layer], g_ple[layer], *ffn_w)
    return h
```

```python
import functools
import math

import numpy as np
import jax
import jax.numpy as jnp
from jax import lax
from jax.experimental import pallas as pl
from jax.experimental.pallas import tpu as pltpu

F32 = jnp.float32
BF16 = jnp.bfloat16

D_MODEL = 1024
D_PLE = 256
EPS = 1e-6

CONV_DIM = 512
GMLP_DIM = 512
GMLP_HEADS = 8
GMLP_HEAD_DIM = 64
GMLP_CHUNK = 128
EVEN_IN = 3 * CONV_DIM + 2 * GMLP_DIM

MLA_HEADS = 8
MLA_NOPE = 64
MLA_ROPE = 32
MLA_V = 64
MLA_QK = MLA_NOPE + MLA_ROPE
Q_LORA = 384
KV_LORA = 256
ROPE_THETA = 10000.0
MLSTM_HEADS = 4
MLSTM_QK = 64
MLSTM_V = 128
D_FF = 2816

LANES = 128
MXU_EDGE = 256

TM = 512
TM_PROJ = 1024
TM_FFN = 512
TQ = 512
TK = 512
KSUB = 256
LOOKAHEAD = 2
ML = 256
V_EXT = MLSTM_V + 16
HEAD_PAD = 128
ODD_IN_PAD = 2304
TAIL_OFF = ODD_IN_PAD - LANES
FF_CHUNKS = ((0, 1024), (1024, 2048), (2048, 2816))
MAX_SCORE_SPREAD = 200.0
SCORE_SCALE = MLA_QK ** -0.5 * math.log2(math.e)
BF16_SLACK = 1.01

VMEM_LIMIT = 56 * 1024 * 1024


def _dot(a, b):
    return jnp.dot(a, b, preferred_element_type=F32)


def _dot_nt(a, b):
    return lax.dot_general(a, b, (((1,), (1,)), ((), ())), preferred_element_type=F32)


def _rms_rows(x, g):
    ms = jnp.mean(x * x, axis=-1, keepdims=True)
    return x * lax.rsqrt(ms + EPS) * g


def _rms_cols(x, g):
    ms = jnp.mean(x * x, axis=0, keepdims=True)
    return x * lax.rsqrt(ms + EPS) * g


def _sigmoid(x):
    return 0.5 * jnp.tanh(0.5 * x) + 0.5


def _const_spec(shape):
    zeros = (0,) * len(shape)
    return pl.BlockSpec(shape, lambda *_: zeros, pipeline_mode=pl.Buffered(1))


def _params(n_axes):
    return pltpu.CompilerParams(dimension_semantics=("arbitrary",) * n_axes,
                                vmem_limit_bytes=VMEM_LIMIT)


def _even_kernel(h_ref, gmix_ref, win_ref, wconv_ref, gmat_ref, gv_ref, wpair_ref, bm_ref,
                 ya_ref, yb_ref, zbuf_ref):
    @pl.when(pl.program_id(1) == 0)
    def _():
        zbuf_ref[0:8, :] = jnp.zeros((8, CONV_DIM), F32)

    lane = lax.broadcasted_iota(jnp.int32, (GMLP_CHUNK, LANES), 1)
    row_t = lax.broadcasted_iota(jnp.int32, (GMLP_CHUNK, 2 * GMLP_CHUNK), 0)
    col_s = lax.broadcasted_iota(jnp.int32, (GMLP_CHUNK, 2 * GMLP_CHUNK), 1) % GMLP_CHUNK
    tril = col_s <= row_t
    wms = [jnp.where(tril, wpair_ref[j], 0.0).astype(BF16) for j in range(GMLP_HEADS // 2)]
    col = lambda i: win_ref[:, i * CONV_DIM:(i + 1) * CONV_DIM]
    wc = wconv_ref[...]

    for s0 in range(0, TM_PROJ, TM):
        hn = _rms_rows(h_ref[0, s0:s0 + TM], gmix_ref[...]).astype(BF16)
        v = _dot(hn, col(4))
        u = _dot(hn, col(3))
        z_cx = _dot(hn, win_ref[:, CONV_DIM:3 * CONV_DIM])
        gv = jax.nn.gelu(v)
        ss = _dot((gv * gv).astype(BF16), gmat_ref[...])
        b_gate = _dot(hn, col(0))

        zz = z_cx[:, 0:CONV_DIM] * z_cx[:, CONV_DIM:2 * CONV_DIM]
        zbuf_ref[8 + s0:8 + s0 + TM, :] = zz
        z1 = zbuf_ref[7 + s0:7 + s0 + TM, :]
        z2 = zbuf_ref[6 + s0:6 + s0 + TM, :]
        conv = wc[2:3] * zz + wc[1:2] * z1 + wc[0:1] * z2
        ya_ref[0, s0:s0 + TM] = (b_gate * conv).astype(BF16)
        if s0 + TM == TM_PROJ:
            zbuf_ref[0:8, :] = zz[TM - 8:TM, :]

        gu = jax.nn.gelu(u)
        vn = gv * lax.rsqrt(ss * (1.0 / GMLP_HEAD_DIM) + EPS) * gv_ref[...]
        for c in range(TM // GMLP_CHUNK):
            r0 = c * GMLP_CHUNK
            outs = []
            for j in range(GMLP_HEADS // 2):
                vp = vn[r0:r0 + GMLP_CHUNK, j * LANES:(j + 1) * LANES]
                lo = jnp.where(lane < GMLP_HEAD_DIM, vp, 0.0).astype(BF16)
                hi = jnp.where(lane >= GMLP_HEAD_DIM, vp, 0.0).astype(BF16)
                outs.append(_dot(wms[j], jnp.concatenate([lo, hi], axis=0)))
            mixed = jnp.concatenate(outs, axis=1) + bm_ref[...]
            yb_ref[0, s0 + r0:s0 + r0 + GMLP_CHUNK, :] = (gu[r0:r0 + GMLP_CHUNK] * mixed).astype(BF16)


def _even_mixer(h, g_mix, w_in, w_conv, g_v, w_s, b_s):
    bsz, seq, _ = h.shape
    gmat = jnp.asarray(np.kron(np.eye(GMLP_HEADS), np.ones((GMLP_HEAD_DIM, GMLP_HEAD_DIM))), BF16)
    wpair = w_s.reshape(GMLP_HEADS // 2, 2, GMLP_CHUNK, GMLP_CHUNK).transpose(0, 2, 1, 3)
    wpair = wpair.reshape(GMLP_HEADS // 2, GMLP_CHUNK, 2 * GMLP_CHUNK)
    bm = jnp.repeat(b_s.T, GMLP_HEAD_DIM, axis=1)
    out_sds = jax.ShapeDtypeStruct((bsz, seq, CONV_DIM), BF16)
    tile = lambda w: pl.BlockSpec((1, TM_PROJ, w), lambda b, i: (b, i, 0))
    return pl.pallas_call(
        _even_kernel,
        grid=(bsz, seq // TM_PROJ),
        in_specs=[tile(D_MODEL), _const_spec((1, D_MODEL)), _const_spec((D_MODEL, EVEN_IN)),
                  _const_spec((3, CONV_DIM)), _const_spec((GMLP_DIM, GMLP_DIM)), _const_spec((1, GMLP_DIM)),
                  _const_spec((GMLP_HEADS // 2, GMLP_CHUNK, 2 * GMLP_CHUNK)),
                  _const_spec((GMLP_CHUNK, GMLP_DIM))],
        out_specs=[tile(CONV_DIM), tile(GMLP_DIM)],
        out_shape=[out_sds, out_sds],
        scratch_shapes=[pltpu.VMEM((TM_PROJ + 8, CONV_DIM), F32)],
        compiler_params=_params(2),
        name="even_mixer",
    )(h, g_mix.reshape(1, D_MODEL), w_in.astype(BF16), w_conv, gmat, g_v.reshape(1, GMLP_DIM), wpair, bm)


def _ffn_kernel(h_ref, y1_ref, y2_ref, p_ref, wout_ref, gffn_ref, wg_ref, wu_ref, wd_ref,
                gple_ref, wpg_ref, wpp_ref, o_ref):
    rows = [slice(r, r + TM_FFN // 2) for r in (0, TM_FFN // 2)]
    p_b = p_ref[0, 0].astype(BF16)
    half = D_MODEL // 2
    xs = [h_ref[0, r] + _dot(jnp.concatenate([y1_ref[0, r], y2_ref[0, r]], axis=-1), wout_ref[...])
          for r in rows]
    pp_lo = _dot(p_b, wpp_ref[:, :half])
    x = jnp.concatenate(xs, axis=0)
    hn = jnp.concatenate([_rms_rows(xr, gffn_ref[...]).astype(BF16) for xr in xs], axis=0)
    acc = x
    for lo, hi in FF_CHUNKS:
        g = _dot(hn, wg_ref[:, lo:hi])
        u = _dot(hn, wu_ref[:, lo:hi])
        a = (g * _sigmoid(g) * u).astype(BF16)
        acc = acc + _dot(a, wd_ref[lo:hi, :])
    pp_hi = _dot(p_b, wpp_ref[:, half:])
    pp = jnp.concatenate([pp_lo, pp_hi], axis=-1)
    for r in rows:
        hn2 = _rms_rows(acc[r], gple_ref[...]).astype(BF16)
        o_ref[0, r] = acc[r] + _sigmoid(_dot(hn2, wpg_ref[...])) * pp[r]


def _ffn_ple(layer, h, y1, y2, p, w_out, g_ffn, g_ple, w_gate, w_up, w_down, w_pg, w_pp):
    bsz, seq, _ = h.shape
    tile = lambda w: pl.BlockSpec((1, TM_FFN, w), lambda b, i: (b, i, 0))
    layer_spec = lambda r, c: pl.BlockSpec((None, r, c), lambda *_: (layer, 0, 0), pipeline_mode=pl.Buffered(1))
    return pl.pallas_call(
        _ffn_kernel,
        grid=(bsz, seq // TM_FFN),
        in_specs=[tile(D_MODEL), tile(512), tile(512),
                  pl.BlockSpec((1, 1, TM_FFN, D_PLE), lambda b, i: (layer, b, i, 0)),
                  _const_spec((D_MODEL, D_MODEL)), _const_spec((1, D_MODEL)),
                  layer_spec(D_MODEL, D_FF), layer_spec(D_MODEL, D_FF), layer_spec(D_FF, D_MODEL),
                  _const_spec((1, D_MODEL)), layer_spec(D_MODEL, D_MODEL), layer_spec(D_PLE, D_MODEL)],
        out_specs=tile(D_MODEL),
        out_shape=jax.ShapeDtypeStruct((bsz, seq, D_MODEL), F32),
        compiler_params=_params(2),
        name="ffn_ple",
    )(h, y1, y2, p, w_out.astype(BF16), g_ffn.reshape(1, D_MODEL), w_gate, w_up, w_down,
      g_ple.reshape(1, D_MODEL), w_pg, w_pp)


def _log_sigmoid(x):
    return jnp.minimum(x, 0.0) - jnp.log1p(jnp.exp(-jnp.abs(x)))


def _rope_cols(xr, cos, sin):
    half = MLA_ROPE // 2
    x1, x2 = xr[:half], xr[half:]
    return x1 * cos - x2 * sin, x2 * cos + x1 * sin


def _odd_kernel(h_ref, pos_ref, gmix_ref, win_ref, bias_ref, gqa_ref, gkva_ref, wqt_ref, wkvt_ref,
                gqn_ref, gqr_ref, gkn_ref, gkr_ref, freq_ref,
                qt_ref, k_ref, vt_ref, spread_ref, mqt_ref, mk_ref, mvt_ref, mo_ref, g_ref, gt_ref):
    for sub in range(TM_PROJ // TM):
        _odd_sub_tile(sub, h_ref, pos_ref, gmix_ref, win_ref, bias_ref, gqa_ref, gkva_ref, wqt_ref, wkvt_ref,
                      gqn_ref, gqr_ref, gkn_ref, gkr_ref, freq_ref,
                      qt_ref, k_ref, vt_ref, spread_ref, mqt_ref, mk_ref, mvt_ref, mo_ref, g_ref, gt_ref)


def _odd_sub_tile(sub, h_ref, pos_ref, gmix_ref, win_ref, bias_ref, gqa_ref, gkva_ref, wqt_ref, wkvt_ref,
                  gqn_ref, gqr_ref, gkn_ref, gkr_ref, freq_ref,
                  qt_ref, k_ref, vt_ref, spread_ref, mqt_ref, mk_ref, mvt_ref, mo_ref, g_ref, gt_ref):
    tok = slice(sub * TM, (sub + 1) * TM)
    hn = _rms_rows(h_ref[0, tok], gmix_ref[...]).astype(BF16)
    z = _dot(hn, win_ref[...])
    q_lat = z[:, 0:384]
    kv_lat = z[:, 384:640]

    mqt_ref[0, :, tok] = (z[:, 640:896] * (MLSTM_QK ** -0.5)).T.astype(BF16)
    mk_ref[0, tok] = z[:, 896:1152].astype(BF16)
    mvt_ref[0, :, tok] = z[:, 1152:1664].T.astype(BF16)
    mo_ref[0, tok] = z[:, 1664:2176]
    tail = z[:, TAIL_OFF:ODD_IN_PAD] + bias_ref[...]
    lane = lax.broadcasted_iota(jnp.int32, (TM, LANES), 1)
    f_lane = (lane >= MLA_ROPE + MLSTM_HEADS) & (lane < MLA_ROPE + 2 * MLSTM_HEADS)
    tail = jnp.where(f_lane, _log_sigmoid(tail), tail)
    g_ref[0, tok] = tail
    tail_t = tail.T
    gt_ref[0, :, tok] = tail_t[MLA_ROPE:MLA_ROPE + 2 * MLSTM_HEADS]

    qn = _rms_rows(q_lat, gqa_ref[...]).astype(BF16)
    kvn = _rms_rows(kv_lat, gkva_ref[...]).astype(BF16)
    q_t = _dot_nt(wqt_ref[...], qn)
    kv_t = _dot_nt(wkvt_ref[...], kvn)

    ang = freq_ref[...] * pos_ref[0, :, tok].astype(F32)
    cos = jnp.cos(ang)
    sin = jnp.sin(ang)
    kr1, kr2 = _rope_cols(_rms_cols(tail_t[0:MLA_ROPE], gkr_ref[...]), cos, sin)

    norm_bound = lambda gn_ref, gr_ref: jnp.sqrt(
        MLA_NOPE * jnp.max(gn_ref[...] ** 2, axis=0, keepdims=True)
        + MLA_ROPE * jnp.max(gr_ref[...] ** 2, axis=0, keepdims=True))
    upper = BF16_SLACK * norm_bound(gqn_ref, gqr_ref) * norm_bound(gkn_ref, gkr_ref)
    first_row = lax.broadcasted_iota(jnp.int32, (8, TM), 0) == 0
    zeros_tail = jnp.zeros((HEAD_PAD - MLA_QK - 8, TM), F32)
    k_pad = jnp.concatenate([jnp.ones((8, TM), F32), zeros_tail], axis=0)
    spreads = []
    for hd in range(MLA_HEADS):
        q0 = hd * MLA_QK
        q_nope = _rms_cols(q_t[q0:q0 + MLA_NOPE], gqn_ref[...])
        qr1, qr2 = _rope_cols(_rms_cols(q_t[q0 + MLA_NOPE:q0 + MLA_QK], gqr_ref[...]), cos, sin)
        q96 = jnp.concatenate([q_nope, qr1, qr2], axis=0)
        k0 = hd * (MLA_NOPE + MLA_V)
        k_nope = _rms_cols(kv_t[k0:k0 + MLA_NOPE], gkn_ref[...])
        k96 = jnp.concatenate([k_nope, kr1, kr2], axis=0)
        lower = jnp.sum(q96 * k96, axis=0, keepdims=True)
        shift = 0.5 * (upper + lower)
        spreads.append(upper - lower)
        q_full = jnp.concatenate([q96, jnp.where(first_row, -shift, 0.0), zeros_tail], axis=0)
        qt_ref[0, hd, :, tok] = q_full.astype(BF16)
        k_ref[0, hd, tok] = jnp.concatenate([k96, k_pad], axis=0).T.astype(BF16)
        vt_ref[0, sub, hd] = kv_t[k0 + MLA_NOPE:k0 + MLA_NOPE + MLA_V].astype(BF16)
    spread_ref[0, :, tok] = jnp.concatenate(spreads, axis=0)


def _odd_proj(h, positions, g_mix, w_in, b_gate, g_qa, g_kva, w_q_up, w_kv_up, g_q, g_k):
    bsz, seq, _ = h.shape
    c = np.cumsum([0, Q_LORA + KV_LORA, MLA_ROPE, 2 * MLSTM_HEADS * (MLSTM_QK + MLSTM_V), 2 * MLSTM_HEADS])
    seg = lambda i: w_in[:, c[i]:c[i + 1]].astype(BF16)
    w_in_r = jnp.zeros((D_MODEL, ODD_IN_PAD), BF16)
    for col0, i in ((0, 0), (c[1], 2), (TAIL_OFF, 1), (TAIL_OFF + MLA_ROPE, 3)):
        w_in_r = lax.dynamic_update_slice(w_in_r, seg(i), (0, col0))
    bias = jnp.zeros((1, LANES), F32).at[0, MLA_ROPE:MLA_ROPE + 2 * MLSTM_HEADS].set(b_gate)
    inv_freq = ROPE_THETA ** (-jnp.arange(0, MLA_ROPE, 2, dtype=F32) / MLA_ROPE)
    col = lambda a: a.reshape(-1, 1)
    tile3 = lambda w: pl.BlockSpec((1, TM_PROJ, w), lambda b, i: (b, i, 0))
    tile3t = lambda r: pl.BlockSpec((1, r, TM_PROJ), lambda b, i: (b, 0, i))
    out_shape = [
        jax.ShapeDtypeStruct((bsz, MLA_HEADS, HEAD_PAD, seq), BF16),
        jax.ShapeDtypeStruct((bsz, MLA_HEADS, seq, HEAD_PAD), BF16),
        jax.ShapeDtypeStruct((bsz, seq // TK, MLA_HEADS, MLA_V, TK), BF16),
        jax.ShapeDtypeStruct((bsz, MLA_HEADS, seq), F32),
        jax.ShapeDtypeStruct((bsz, 256, seq), BF16),
        jax.ShapeDtypeStruct((bsz, seq, 256), BF16),
        jax.ShapeDtypeStruct((bsz, 512, seq), BF16),
        jax.ShapeDtypeStruct((bsz, seq, 512), F32),
        jax.ShapeDtypeStruct((bsz, seq, LANES), F32),
        jax.ShapeDtypeStruct((bsz, 2 * MLSTM_HEADS, seq), F32),
    ]
    out_specs = [
        pl.BlockSpec((1, MLA_HEADS, HEAD_PAD, TM_PROJ), lambda b, i: (b, 0, 0, i)),
        pl.BlockSpec((1, MLA_HEADS, TM_PROJ, HEAD_PAD), lambda b, i: (b, 0, i, 0)),
        pl.BlockSpec((1, TM_PROJ // TK, MLA_HEADS, MLA_V, TK), lambda b, i: (b, i, 0, 0, 0)),
        tile3t(MLA_HEADS), tile3t(256), tile3(256), tile3t(512), tile3(512), tile3(LANES),
        tile3t(2 * MLSTM_HEADS),
    ]
    assert TM == TK, "one v^T key tile is written per sub-tile"
    return pl.pallas_call(
        _odd_kernel,
        grid=(bsz, seq // TM_PROJ),
        in_specs=[tile3(D_MODEL), tile3t(1),
                  _const_spec((1, D_MODEL)), _const_spec((D_MODEL, ODD_IN_PAD)), _const_spec((1, LANES)),
                  _const_spec((1, Q_LORA)), _const_spec((1, KV_LORA)),
                  _const_spec((MLA_HEADS * MLA_QK, Q_LORA)), _const_spec((MLA_HEADS * (MLA_NOPE + MLA_V), KV_LORA)),
                  _const_spec((MLA_NOPE, 1)), _const_spec((MLA_ROPE, 1)),
                  _const_spec((MLA_NOPE, 1)), _const_spec((MLA_ROPE, 1)), _const_spec((MLA_ROPE // 2, 1))],
        out_specs=out_specs,
        out_shape=out_shape,
        compiler_params=_params(2),
        name="odd_proj",
    )(h, positions.reshape(bsz, 1, seq), g_mix.reshape(1, D_MODEL), w_in_r, bias,
      g_qa.reshape(1, Q_LORA), g_kva.reshape(1, KV_LORA), w_q_up.T.astype(BF16), w_kv_up.T.astype(BF16),
      col(g_q[:MLA_NOPE] * SCORE_SCALE), col(g_q[MLA_NOPE:] * SCORE_SCALE),
      col(g_k[:MLA_NOPE]), col(g_k[MLA_NOPE:]),
      inv_freq.reshape(-1, 1))


def _attn_kernel(qt_ref, k_ref, vt_ref, o_ref, m_ref, l_ref, acc_ref, kbuf_ref, vbuf_ref, *, running_max):
    qi = pl.program_id(1)
    m_ref[...] = jnp.full(m_ref.shape, -jnp.inf, F32)
    l_ref[...] = jnp.zeros(l_ref.shape, F32)
    acc_ref[...] = jnp.zeros(acc_ref.shape, F32)
    kbuf_ref[qi] = k_ref[0]
    vbuf_ref[qi] = vt_ref[0, 0]

    units = [(hd, part) for hd in range(MLA_HEADS) for part in range(TK // KSUB)]

    def scores(j, unit, t0):
        hd, part = unit
        k = kbuf_ref[j, hd, part * KSUB:(part + 1) * KSUB, :]
        return _dot(k, qt_ref[0, hd, :, t0:])

    def kv_tile(j, masked):
        first_q = lambda unit: unit[1] * KSUB if masked else 0
        pending = [scores(j, u, first_q(u)) for u in units[:LOOKAHEAD]]
        for i, (hd, part) in enumerate(units):
            s = pending.pop(0)
            if i + LOOKAHEAD < len(units):
                nxt = units[i + LOOKAHEAD]
                pending.append(scores(j, nxt, first_q(nxt)))
            t0 = first_q((hd, part))
            if masked:
                row_s = lax.broadcasted_iota(jnp.int32, s.shape, 0) + part * KSUB
                col_t = lax.broadcasted_iota(jnp.int32, s.shape, 1) + t0
                s = jnp.where(row_s <= col_t, s, -jnp.inf)
            v_t = vbuf_ref[j, hd, :, part * KSUB:(part + 1) * KSUB]
            if running_max:
                m_old = m_ref[hd, :, t0:]
                m_new = jnp.maximum(m_old, jnp.max(s, axis=0, keepdims=True))
                alpha = jnp.exp2(m_old - m_new)
                p = jnp.exp2(s - m_new)
                m_ref[hd, :, t0:] = m_new
                l_ref[hd, :, t0:] = alpha * l_ref[hd, :, t0:] + jnp.sum(p, axis=0, keepdims=True)
                acc_ref[hd, :, t0:] = alpha * acc_ref[hd, :, t0:] + _dot(v_t, p.astype(BF16))
            else:
                p = jnp.exp2(s)
                l_ref[hd, :, t0:] += jnp.sum(p, axis=0, keepdims=True)
                acc_ref[hd, :, t0:] += _dot(v_t, p.astype(BF16))

    def body(j, _):
        kv_tile(j, False)
        return 0

    lax.fori_loop(0, qi, body, 0)
    kv_tile(qi, True)
    out_t = jnp.concatenate([acc_ref[hd] / l_ref[hd] for hd in range(MLA_HEADS)], axis=0)
    o_ref[0] = out_t.T.astype(BF16)


def _attention(qt, k, vt, *, running_max):
    bsz, _, _, seq = qt.shape
    return pl.pallas_call(
        functools.partial(_attn_kernel, running_max=running_max),
        grid=(bsz, seq // TQ),
        in_specs=[pl.BlockSpec((1, MLA_HEADS, HEAD_PAD, TQ), lambda b, i: (b, 0, 0, i)),
                  pl.BlockSpec((1, MLA_HEADS, TK, HEAD_PAD), lambda b, i: (b, 0, i, 0)),
                  pl.BlockSpec((1, 1, MLA_HEADS, MLA_V, TK), lambda b, i: (b, i, 0, 0, 0))],
        out_specs=pl.BlockSpec((1, TQ, MLA_HEADS * MLA_V), lambda b, i: (b, i, 0)),
        out_shape=jax.ShapeDtypeStruct((bsz, seq, MLA_HEADS * MLA_V), BF16),
        scratch_shapes=[pltpu.VMEM((MLA_HEADS, 1, TQ), F32), pltpu.VMEM((MLA_HEADS, 1, TQ), F32),
                        pltpu.VMEM((MLA_HEADS, MLA_V, TQ), F32),
                        pltpu.VMEM((seq // TK, MLA_HEADS, TK, HEAD_PAD), BF16),
                        pltpu.VMEM((seq // TK, MLA_HEADS, MLA_V, TK), BF16)],
        compiler_params=_params(2),
        name="attention_running_max" if running_max else "attention",
    )(qt, k, vt)


def _split3(x):
    hi = x.astype(BF16)
    r = x - hi.astype(F32)
    mid = r.astype(BF16)
    lo = (r - mid.astype(F32)).astype(BF16)
    return hi, mid, lo


def _mlstm_kernel(mqt_ref, mk_ref, mvt_ref, mo_ref, g_ref, gt_ref, gmh_ref, yd_ref, ct_ref, m_ref):
    bsz = mqt_ref.shape[0]

    @pl.when(pl.program_id(0) == 0)
    def _():
        ct_ref[...] = jnp.zeros(ct_ref.shape, F32)
        m_ref[...] = jnp.zeros(m_ref.shape, F32)

    row_s = lax.broadcasted_iota(jnp.int32, (ML, ML), 0)
    col_t = lax.broadcasted_iota(jnp.int32, (ML, ML), 1)
    visible = row_s <= col_t
    upper = jnp.where(visible, 1.0, 0.0).astype(BF16)
    lower = jnp.where(col_t <= row_s, 1.0, 0.0).astype(BF16)
    lane = lax.broadcasted_iota(jnp.int32, (ML, LANES), 1)
    k_lo = lane < MLSTM_QK
    ones_rows = jnp.where(lax.broadcasted_iota(jnp.int32, (V_EXT - MLSTM_V, ML), 0) == 0, 1.0, 0.0)
    i_lane0 = MLA_ROPE

    chains = [(b, hd) for b in range(bsz) for hd in range(MLSTM_HEADS)]
    a_row, i_row, c_col = {}, {}, {}
    for b in range(bsz):
        gcol = g_ref[b]
        grow = gt_ref[b]
        f_al = pltpu.roll(gcol, LANES - MLSTM_HEADS, axis=1)
        a_al = sum(_dot(lower, piece) for piece in _split3(f_al))
        a_rows = sum(_dot(piece, upper) for piece in _split3(jnp.concatenate([grow, grow], axis=0)))
        c_all = gcol - a_al
        for hd in range(MLSTM_HEADS):
            a_row[b, hd] = a_rows[MLSTM_HEADS + hd:MLSTM_HEADS + hd + 1]
            i_row[b, hd] = grow[hd:hd + 1]
            c_col[b, hd] = c_all[:, i_lane0 + hd:i_lane0 + hd + 1]

    k_m, qt_pair, vt_ext, sqk, qc = {}, {}, {}, {}, {}
    for b, hd in chains:
        pair, half = hd // 2, hd % 2
        k_pair = mk_ref[b, :, pair * LANES:(pair + 1) * LANES]
        k_m[b, hd] = jnp.where(k_lo if half == 0 else ~k_lo, k_pair.astype(F32), 0.0).astype(BF16)
        qt_pair[b, hd] = mqt_ref[b, pair * LANES:(pair + 1) * LANES, :]
        sqk[b, hd] = _dot(k_m[b, hd], qt_pair[b, hd])
    for b, hd in chains:
        idx = b * MLSTM_HEADS + hd
        qc[b, hd] = _dot(ct_ref[idx].astype(BF16), qt_pair[b, hd])

    m_t, m_new, m_prev, w_inter, s_t, den_intra = {}, {}, {}, {}, {}, {}
    for b, hd in chains:
        idx = b * MLSTM_HEADS + hd
        m_prev[b, hd] = m_ref[idx][0:1, 0:1]
        d_t = jnp.where(visible, a_row[b, hd] + c_col[b, hd], -jnp.inf)
        inter = a_row[b, hd] + m_prev[b, hd]
        m_t[b, hd] = jnp.maximum(inter, jnp.max(d_t, axis=0, keepdims=True))
        w_inter[b, hd] = jnp.exp(inter - m_t[b, hd])
        s_t[b, hd] = sqk[b, hd] * jnp.exp(d_t - m_t[b, hd])
        den_intra[b, hd] = jnp.sum(s_t[b, hd], axis=0, keepdims=True)

    nv = {}
    for b, hd in chains:
        v_t = mvt_ref[b, hd * MLSTM_V:(hd + 1) * MLSTM_V, :]
        vt_ext[b, hd] = jnp.concatenate([v_t.astype(F32), ones_rows], axis=0)
        nv[b, hd] = _dot(vt_ext[b, hd].astype(BF16), s_t[b, hd].astype(BF16))

    for b, hd in chains:
        idx = b * MLSTM_HEADS + hd
        a_last = a_row[b, hd][:, ML - 1:ML]
        m_new[b, hd] = m_t[b, hd][:, ML - 1:ML]
        w_s = jnp.exp(a_last - a_row[b, hd] + i_row[b, hd] - m_new[b, hd])
        decay = jnp.exp(a_last + m_prev[b, hd] - m_new[b, hd])
        upd = _dot((vt_ext[b, hd] * w_s).astype(BF16), k_m[b, hd])
        ct_ref[idx] = decay * ct_ref[idx] + upd
        m_ref[idx] = jnp.broadcast_to(m_new[b, hd], (8, LANES))

    for b in range(bsz):
        h_t = []
        for hd in range(MLSTM_HEADS):
            num = nv[b, hd][:MLSTM_V] + w_inter[b, hd] * qc[b, hd][:MLSTM_V]
            den = den_intra[b, hd] + w_inter[b, hd] * qc[b, hd][MLSTM_V:MLSTM_V + 1]
            hh = num / jnp.maximum(jnp.abs(den), jnp.exp(-m_t[b, hd]))
            h_t.append(hh * lax.rsqrt(jnp.mean(hh * hh, axis=0, keepdims=True) + EPS))
        hn = jnp.concatenate(h_t, axis=0).T * gmh_ref[...]
        yd_ref[b] = (_sigmoid(mo_ref[b]) * hn).astype(BF16)


def _mlstm(mqt, mk, mvt, mo, g, gt, g_mh):
    bsz, seq, _ = mk.shape
    tile3 = lambda w: pl.BlockSpec((bsz, ML, w), lambda i: (0, i, 0))
    tile3t = lambda r: pl.BlockSpec((bsz, r, ML), lambda i: (0, 0, i))
    return pl.pallas_call(
        _mlstm_kernel,
        grid=(seq // ML,),
        in_specs=[tile3t(256), tile3(256), tile3t(512), tile3(512),
                  tile3(LANES), tile3t(2 * MLSTM_HEADS), _const_spec((1, MLSTM_HEADS * MLSTM_V))],
        out_specs=tile3(512),
        out_shape=jax.ShapeDtypeStruct((bsz, seq, MLSTM_HEADS * MLSTM_V), BF16),
        scratch_shapes=[pltpu.VMEM((bsz * MLSTM_HEADS, V_EXT, 2 * MLSTM_QK), F32),
                        pltpu.VMEM((bsz * MLSTM_HEADS, 8, LANES), F32)],
        compiler_params=_params(1),
        name="mlstm",
    )(mqt, mk, mvt, mo, g, gt, g_mh.reshape(1, MLSTM_HEADS * MLSTM_V))


def kernel(x, p, positions, g_mix, g_ffn, g_ple, ev_w_in, ev_w_conv, ev_g_v, ev_w_s, ev_b_s, ev_w_out,
           od_w_in, od_b_gate, od_g_qa, od_g_kva, od_w_q_up, od_w_kv_up, od_g_q, od_g_k, od_g_mh, od_w_out,
           w_gate, w_up, w_down, w_ple_proj, w_ple_gate):
    h = x
    depth = g_mix.shape[0]
    ffn_w = tuple(w.astype(BF16) for w in (w_gate, w_up, w_down, w_ple_gate, w_ple_proj))
    for layer in range(depth):
        j = layer // 2
        if layer % 2 == 0:
            y1, y2 = _even_mixer(h, g_mix[layer], ev_w_in[j], ev_w_conv[j], ev_g_v[j], ev_w_s[j], ev_b_s[j])
            w_out = ev_w_out[j]
        else:
            qt, k, vt, spread, mqt, mk, mvt, mo, g, gt = _odd_proj(
                h, positions, g_mix[layer], od_w_in[j], od_b_gate[j], od_g_qa[j], od_g_kva[j],
                od_w_q_up[j], od_w_kv_up[j], od_g_q[j], od_g_k[j])
            y1 = lax.cond(jnp.max(spread) <= MAX_SCORE_SPREAD,
                          functools.partial(_attention, running_max=False),
                          functools.partial(_attention, running_max=True), qt, k, vt)
            y2 = _mlstm(mqt, mk, mvt, mo, g, gt, od_g_mh[j])
            w_out = od_w_out[j]
        h = _ffn_ple(layer, h, y1, y2, p, w_out, g_ffn[layer], g_ple[layer], *ffn_w)
    return h
```

```python
import functools
import math

import numpy as np
import jax
import jax.numpy as jnp
from jax import lax
from jax.experimental import pallas as pl
from jax.experimental.pallas import tpu as pltpu

F32 = jnp.float32
BF16 = jnp.bfloat16

D_MODEL = 1024
D_PLE = 256
EPS = 1e-6

CONV_DIM = 512
GMLP_DIM = 512
GMLP_HEADS = 8
GMLP_HEAD_DIM = 64
GMLP_CHUNK = 128
EVEN_IN = 3 * CONV_DIM + 2 * GMLP_DIM

MLA_HEADS = 8
MLA_NOPE = 64
MLA_ROPE = 32
MLA_V = 64
MLA_QK = MLA_NOPE + MLA_ROPE
Q_LORA = 384
KV_LORA = 256
ROPE_THETA = 10000.0
MLSTM_HEADS = 4
MLSTM_QK = 64
MLSTM_V = 128
D_FF = 2816

LANES = 128
MXU_EDGE = 256

TM = 512
TM_PROJ = 1024
TM_ODD = 512
TM_FFN = 512
TQ = 512
TK = 512
KSUB = 256
LOOKAHEAD = 2
ML = 256
V_EXT = MLSTM_V + 16
HEAD_PAD = 128
ODD_IN_PAD = 2304
TAIL_OFF = ODD_IN_PAD - LANES
FF_CHUNKS = ((0, 1024), (1024, 2048), (2048, 2816))
MAX_SCORE_SPREAD = 200.0
SCORE_SCALE = MLA_QK ** -0.5 * math.log2(math.e)
BF16_SLACK = 1.01

VMEM_LIMIT = 56 * 1024 * 1024


def _dot(a, b):
    return jnp.dot(a, b, preferred_element_type=F32)


def _dot_nt(a, b):
    return lax.dot_general(a, b, (((1,), (1,)), ((), ())), preferred_element_type=F32)


def _rms_rows(x, g):
    ms = jnp.mean(x * x, axis=-1, keepdims=True)
    return x * lax.rsqrt(ms + EPS) * g


def _rms_cols(x, g):
    ms = jnp.mean(x * x, axis=0, keepdims=True)
    return x * lax.rsqrt(ms + EPS) * g


def _sigmoid(x):
    return 0.5 * jnp.tanh(0.5 * x) + 0.5


def _const_spec(shape):
    zeros = (0,) * len(shape)
    return pl.BlockSpec(shape, lambda *_: zeros, pipeline_mode=pl.Buffered(1))


def _params(n_axes):
    return pltpu.CompilerParams(dimension_semantics=("arbitrary",) * n_axes,
                                vmem_limit_bytes=VMEM_LIMIT)


def _even_kernel(h_ref, gmix_ref, win_ref, wconv_ref, gmat_ref, gv_ref, wpair_ref, bm_ref,
                 ya_ref, yb_ref, zbuf_ref):
    @pl.when(pl.program_id(1) == 0)
    def _():
        zbuf_ref[0:8, :] = jnp.zeros((8, CONV_DIM), F32)

    lane = lax.broadcasted_iota(jnp.int32, (GMLP_CHUNK, LANES), 1)
    row_t = lax.broadcasted_iota(jnp.int32, (GMLP_CHUNK, 2 * GMLP_CHUNK), 0)
    col_s = lax.broadcasted_iota(jnp.int32, (GMLP_CHUNK, 2 * GMLP_CHUNK), 1) % GMLP_CHUNK
    tril = col_s <= row_t
    wms = [jnp.where(tril, wpair_ref[j], 0.0).astype(BF16) for j in range(GMLP_HEADS // 2)]
    col = lambda i: win_ref[:, i * CONV_DIM:(i + 1) * CONV_DIM]
    wc = wconv_ref[...]

    for s0 in range(0, TM_PROJ, TM):
        hn = _rms_rows(h_ref[0, s0:s0 + TM], gmix_ref[...]).astype(BF16)
        v = _dot(hn, col(4))
        u = _dot(hn, col(3))
        z_cx = _dot(hn, win_ref[:, CONV_DIM:3 * CONV_DIM])
        gv = jax.nn.gelu(v)
        ss = _dot((gv * gv).astype(BF16), gmat_ref[...])
        b_gate = _dot(hn, col(0))

        zz = z_cx[:, 0:CONV_DIM] * z_cx[:, CONV_DIM:2 * CONV_DIM]
        zbuf_ref[8 + s0:8 + s0 + TM, :] = zz
        z1 = zbuf_ref[7 + s0:7 + s0 + TM, :]
        z2 = zbuf_ref[6 + s0:6 + s0 + TM, :]
        conv = wc[2:3] * zz + wc[1:2] * z1 + wc[0:1] * z2
        ya_ref[0, s0:s0 + TM] = (b_gate * conv).astype(BF16)
        if s0 + TM == TM_PROJ:
            zbuf_ref[0:8, :] = zz[TM - 8:TM, :]

        gu = jax.nn.gelu(u)
        vn = gv * lax.rsqrt(ss * (1.0 / GMLP_HEAD_DIM) + EPS) * gv_ref[...]
        for c in range(TM // GMLP_CHUNK):
            r0 = c * GMLP_CHUNK
            outs = []
            for j in range(GMLP_HEADS // 2):
                vp = vn[r0:r0 + GMLP_CHUNK, j * LANES:(j + 1) * LANES]
                lo = jnp.where(lane < GMLP_HEAD_DIM, vp, 0.0).astype(BF16)
                hi = jnp.where(lane >= GMLP_HEAD_DIM, vp, 0.0).astype(BF16)
                outs.append(_dot(wms[j], jnp.concatenate([lo, hi], axis=0)))
            mixed = jnp.concatenate(outs, axis=1) + bm_ref[...]
            yb_ref[0, s0 + r0:s0 + r0 + GMLP_CHUNK, :] = (gu[r0:r0 + GMLP_CHUNK] * mixed).astype(BF16)


def _even_mixer(h, g_mix, w_in, w_conv, g_v, w_s, b_s):
    bsz, seq, _ = h.shape
    gmat = jnp.asarray(np.kron(np.eye(GMLP_HEADS), np.ones((GMLP_HEAD_DIM, GMLP_HEAD_DIM))), BF16)
    wpair = w_s.reshape(GMLP_HEADS // 2, 2, GMLP_CHUNK, GMLP_CHUNK).transpose(0, 2, 1, 3)
    wpair = wpair.reshape(GMLP_HEADS // 2, GMLP_CHUNK, 2 * GMLP_CHUNK)
    bm = jnp.repeat(b_s.T, GMLP_HEAD_DIM, axis=1)
    out_sds = jax.ShapeDtypeStruct((bsz, seq, CONV_DIM), BF16)
    tile = lambda w: pl.BlockSpec((1, TM_PROJ, w), lambda b, i: (b, i, 0))
    return pl.pallas_call(
        _even_kernel,
        grid=(bsz, seq // TM_PROJ),
        in_specs=[tile(D_MODEL), _const_spec((1, D_MODEL)), _const_spec((D_MODEL, EVEN_IN)),
                  _const_spec((3, CONV_DIM)), _const_spec((GMLP_DIM, GMLP_DIM)), _const_spec((1, GMLP_DIM)),
                  _const_spec((GMLP_HEADS // 2, GMLP_CHUNK, 2 * GMLP_CHUNK)),
                  _const_spec((GMLP_CHUNK, GMLP_DIM))],
        out_specs=[tile(CONV_DIM), tile(GMLP_DIM)],
        out_shape=[out_sds, out_sds],
        scratch_shapes=[pltpu.VMEM((TM_PROJ + 8, CONV_DIM), F32)],
        compiler_params=_params(2),
        name="even_mixer",
    )(h, g_mix.reshape(1, D_MODEL), w_in.astype(BF16), w_conv, gmat, g_v.reshape(1, GMLP_DIM), wpair, bm)


def _ffn_kernel(h_ref, y1_ref, y2_ref, p_ref, wout_ref, gffn_ref, wg_ref, wu_ref, wd_ref,
                gple_ref, wpg_ref, wpp_ref, o_ref):
    rows = [slice(r, r + TM_FFN // 2) for r in (0, TM_FFN // 2)]
    p_b = p_ref[0, 0].astype(BF16)
    half = D_MODEL // 2
    xs = [h_ref[0, r] + _dot(jnp.concatenate([y1_ref[0, r], y2_ref[0, r]], axis=-1), wout_ref[...])
          for r in rows]
    pp_lo = _dot(p_b, wpp_ref[:, :half])
    x = jnp.concatenate(xs, axis=0)
    hn = jnp.concatenate([_rms_rows(xr, gffn_ref[...]).astype(BF16) for xr in xs], axis=0)
    acc = x
    for lo, hi in FF_CHUNKS:
        g = _dot(hn, wg_ref[:, lo:hi])
        u = _dot(hn, wu_ref[:, lo:hi])
        a = (g * _sigmoid(g) * u).astype(BF16)
        acc = acc + _dot(a, wd_ref[lo:hi, :])
    pp_hi = _dot(p_b, wpp_ref[:, half:])
    pp = jnp.concatenate([pp_lo, pp_hi], axis=-1)
    for r in rows:
        hn2 = _rms_rows(acc[r], gple_ref[...]).astype(BF16)
        o_ref[0, r] = acc[r] + _sigmoid(_dot(hn2, wpg_ref[...])) * pp[r]


def _ffn_ple(layer, h, y1, y2, p, w_out, g_ffn, g_ple, w_gate, w_up, w_down, w_pg, w_pp):
    bsz, seq, _ = h.shape
    tile = lambda w: pl.BlockSpec((1, TM_FFN, w), lambda b, i: (b, i, 0))
    layer_spec = lambda r, c: pl.BlockSpec((None, r, c), lambda *_: (layer, 0, 0), pipeline_mode=pl.Buffered(1))
    return pl.pallas_call(
        _ffn_kernel,
        grid=(bsz, seq // TM_FFN),
        in_specs=[tile(D_MODEL), tile(512), tile(512),
                  pl.BlockSpec((1, 1, TM_FFN, D_PLE), lambda b, i: (layer, b, i, 0)),
                  _const_spec((D_MODEL, D_MODEL)), _const_spec((1, D_MODEL)),
                  layer_spec(D_MODEL, D_FF), layer_spec(D_MODEL, D_FF), layer_spec(D_FF, D_MODEL),
                  _const_spec((1, D_MODEL)), layer_spec(D_MODEL, D_MODEL), layer_spec(D_PLE, D_MODEL)],
        out_specs=tile(D_MODEL),
        out_shape=jax.ShapeDtypeStruct((bsz, seq, D_MODEL), F32),
        compiler_params=_params(2),
        name="ffn_ple",
    )(h, y1, y2, p, w_out.astype(BF16), g_ffn.reshape(1, D_MODEL), w_gate, w_up, w_down,
      g_ple.reshape(1, D_MODEL), w_pg, w_pp)


def _log_sigmoid(x):
    return jnp.minimum(x, 0.0) - jnp.log1p(jnp.exp(-jnp.abs(x)))


def _rope_cols(xr, cos, sin):
    half = MLA_ROPE // 2
    x1, x2 = xr[:half], xr[half:]
    return x1 * cos - x2 * sin, x2 * cos + x1 * sin


def _join_kernel(wlat_ref, wml_ref, wtail_ref, win_ref):
    win_ref[:, 0:Q_LORA + KV_LORA] = wlat_ref[...]
    win_ref[:, Q_LORA + KV_LORA:TAIL_OFF] = wml_ref[...]
    win_ref[:, TAIL_OFF:ODD_IN_PAD] = wtail_ref[...]


def _odd_kernel(h_ref, pos_ref, gmix_ref, win_ref, bias_ref, gqa_ref, gkva_ref, wqt_ref,
                wkvt_ref, gqn_ref, gqr_ref, gkn_ref, gkr_ref, freq_ref,
                qt_ref, k_ref, vt_ref, spread_ref, mqt_ref, mk_ref, mvt_ref, mo_ref, g_ref, gt_ref):
    for sub in range(TM_ODD // TM):
        _odd_sub_tile(sub, h_ref, pos_ref, gmix_ref, win_ref, bias_ref, gqa_ref, gkva_ref, wqt_ref, wkvt_ref,
                      gqn_ref, gqr_ref, gkn_ref, gkr_ref, freq_ref,
                      qt_ref, k_ref, vt_ref, spread_ref, mqt_ref, mk_ref, mvt_ref, mo_ref, g_ref, gt_ref)


def _odd_sub_tile(sub, h_ref, pos_ref, gmix_ref, win_ref, bias_ref, gqa_ref, gkva_ref, wqt_ref, wkvt_ref,
                  gqn_ref, gqr_ref, gkn_ref, gkr_ref, freq_ref,
                  qt_ref, k_ref, vt_ref, spread_ref, mqt_ref, mk_ref, mvt_ref, mo_ref, g_ref, gt_ref):
    tok = slice(sub * TM, (sub + 1) * TM)
    hn = _rms_rows(h_ref[0, tok], gmix_ref[...]).astype(BF16)
    z = _dot(hn, win_ref[...])
    q_lat = z[:, 0:384]
    kv_lat = z[:, 384:640]

    mqt_ref[0, :, tok] = (z[:, 640:896] * (MLSTM_QK ** -0.5)).T.astype(BF16)
    mk_ref[0, tok] = z[:, 896:1152].astype(BF16)
    mvt_ref[0, :, tok] = z[:, 1152:1664].T.astype(BF16)
    mo_ref[0, tok] = z[:, 1664:2176]
    tail = z[:, TAIL_OFF:ODD_IN_PAD] + bias_ref[...]
    lane = lax.broadcasted_iota(jnp.int32, (TM, LANES), 1)
    f_lane = (lane >= MLA_ROPE + MLSTM_HEADS) & (lane < MLA_ROPE + 2 * MLSTM_HEADS)
    tail = jnp.where(f_lane, _log_sigmoid(tail), tail)
    g_ref[0, tok] = tail
    tail_t = tail.T
    gt_ref[0, :, tok] = tail_t[MLA_ROPE:MLA_ROPE + 2 * MLSTM_HEADS]

    qn = _rms_rows(q_lat, gqa_ref[...]).astype(BF16)
    kvn = _rms_rows(kv_lat, gkva_ref[...]).astype(BF16)
    q_t = _dot_nt(wqt_ref[...], qn)
    kv_t = _dot_nt(wkvt_ref[...], kvn)

    ang = freq_ref[...] * pos_ref[0, :, tok].astype(F32)
    cos = jnp.cos(ang)
    sin = jnp.sin(ang)
    kr1, kr2 = _rope_cols(_rms_cols(tail_t[0:MLA_ROPE], gkr_ref[...]), cos, sin)

    norm_bound = lambda gn_ref, gr_ref: jnp.sqrt(
        MLA_NOPE * jnp.max(gn_ref[...] ** 2, axis=0, keepdims=True)
        + MLA_ROPE * jnp.max(gr_ref[...] ** 2, axis=0, keepdims=True))
    upper = BF16_SLACK * norm_bound(gqn_ref, gqr_ref) * norm_bound(gkn_ref, gkr_ref)
    first_row = lax.broadcasted_iota(jnp.int32, (8, TM), 0) == 0
    zeros_tail = jnp.zeros((HEAD_PAD - MLA_QK - 8, TM), F32)
    k_pad = jnp.concatenate([jnp.ones((8, TM), F32), zeros_tail], axis=0)
    spreads = []
    for hd in range(MLA_HEADS):
        q0 = hd * MLA_QK
        q_nope = _rms_cols(q_t[q0:q0 + MLA_NOPE], gqn_ref[...])
        qr1, qr2 = _rope_cols(_rms_cols(q_t[q0 + MLA_NOPE:q0 + MLA_QK], gqr_ref[...]), cos, sin)
        q96 = jnp.concatenate([q_nope, qr1, qr2], axis=0)
        k0 = hd * (MLA_NOPE + MLA_V)
        k_nope = _rms_cols(kv_t[k0:k0 + MLA_NOPE], gkn_ref[...])
        k96 = jnp.concatenate([k_nope, kr1, kr2], axis=0)
        lower = jnp.sum(q96 * k96, axis=0, keepdims=True)
        shift = 0.5 * (upper + lower)
        spreads.append(upper - lower)
        q_full = jnp.concatenate([q96, jnp.where(first_row, -shift, 0.0), zeros_tail], axis=0)
        qt_ref[0, hd, :, tok] = q_full.astype(BF16)
        k_ref[0, hd, tok] = jnp.concatenate([k96, k_pad], axis=0).T.astype(BF16)
        vt_ref[0, sub, hd] = kv_t[k0 + MLA_NOPE:k0 + MLA_NOPE + MLA_V].astype(BF16)
    spread_ref[0, :, tok] = jnp.concatenate(spreads, axis=0)


def _odd_proj(h, positions, g_mix, w_in, b_gate, g_qa, g_kva, w_q_up, w_kv_up, g_q, g_k):
    bsz, seq, _ = h.shape
    c = np.cumsum([0, Q_LORA + KV_LORA, MLA_ROPE, 2 * MLSTM_HEADS * (MLSTM_QK + MLSTM_V), 2 * MLSTM_HEADS])
    seg = lambda i: w_in[:, c[i]:c[i + 1]].astype(BF16)
    tail_pad = jnp.zeros((D_MODEL, LANES - MLA_ROPE - 2 * MLSTM_HEADS), BF16)
    groups = (seg(0), seg(2), jnp.concatenate([seg(1), seg(3), tail_pad], axis=1))
    w_in_r = pl.pallas_call(
        _join_kernel,
        out_shape=jax.ShapeDtypeStruct((D_MODEL, ODD_IN_PAD), BF16),
        compiler_params=pltpu.CompilerParams(vmem_limit_bytes=VMEM_LIMIT),
        name="odd_w_in_join",
    )(*groups)
    bias = jnp.zeros((1, LANES), F32).at[0, MLA_ROPE:MLA_ROPE + 2 * MLSTM_HEADS].set(b_gate)
    inv_freq = ROPE_THETA ** (-jnp.arange(0, MLA_ROPE, 2, dtype=F32) / MLA_ROPE)
    col = lambda a: a.reshape(-1, 1)
    tile3 = lambda w: pl.BlockSpec((1, TM_ODD, w), lambda b, i: (b, i, 0))
    tile3t = lambda r: pl.BlockSpec((1, r, TM_ODD), lambda b, i: (b, 0, i))
    out_shape = [
        jax.ShapeDtypeStruct((bsz, MLA_HEADS, HEAD_PAD, seq), BF16),
        jax.ShapeDtypeStruct((bsz, MLA_HEADS, seq, HEAD_PAD), BF16),
        jax.ShapeDtypeStruct((bsz, seq // TK, MLA_HEADS, MLA_V, TK), BF16),
        jax.ShapeDtypeStruct((bsz, MLA_HEADS, seq), F32),
        jax.ShapeDtypeStruct((bsz, 256, seq), BF16),
        jax.ShapeDtypeStruct((bsz, seq, 256), BF16),
        jax.ShapeDtypeStruct((bsz, 512, seq), BF16),
        jax.ShapeDtypeStruct((bsz, seq, 512), F32),
        jax.ShapeDtypeStruct((bsz, seq, LANES), F32),
        jax.ShapeDtypeStruct((bsz, 2 * MLSTM_HEADS, seq), F32),
    ]
    out_specs = [
        pl.BlockSpec((1, MLA_HEADS, HEAD_PAD, TM_ODD), lambda b, i: (b, 0, 0, i)),
        pl.BlockSpec((1, MLA_HEADS, TM_ODD, HEAD_PAD), lambda b, i: (b, 0, i, 0)),
        pl.BlockSpec((1, TM_ODD // TK, MLA_HEADS, MLA_V, TK), lambda b, i: (b, i, 0, 0, 0)),
        tile3t(MLA_HEADS), tile3t(256), tile3(256), tile3t(512), tile3(512), tile3(LANES),
        tile3t(2 * MLSTM_HEADS),
    ]
    assert TM == TK, "one v^T key tile is written per sub-tile"
    return pl.pallas_call(
        _odd_kernel,
        grid=(bsz, seq // TM_ODD),
        in_specs=[tile3(D_MODEL), tile3t(1),
                  _const_spec((1, D_MODEL)), _const_spec((D_MODEL, ODD_IN_PAD)), _const_spec((1, LANES)),
                  _const_spec((1, Q_LORA)), _const_spec((1, KV_LORA)),
                  _const_spec((MLA_HEADS * MLA_QK, Q_LORA)), _const_spec((MLA_HEADS * (MLA_NOPE + MLA_V), KV_LORA)),
                  _const_spec((MLA_NOPE, 1)), _const_spec((MLA_ROPE, 1)),
                  _const_spec((MLA_NOPE, 1)), _const_spec((MLA_ROPE, 1)), _const_spec((MLA_ROPE // 2, 1))],
        out_specs=out_specs,
        out_shape=out_shape,
        compiler_params=_params(2),
        name="odd_proj",
    )(h, positions.reshape(bsz, 1, seq), g_mix.reshape(1, D_MODEL), w_in_r, bias,
      g_qa.reshape(1, Q_LORA), g_kva.reshape(1, KV_LORA), w_q_up.T.astype(BF16), w_kv_up.T.astype(BF16),
      col(g_q[:MLA_NOPE] * SCORE_SCALE), col(g_q[MLA_NOPE:] * SCORE_SCALE),
      col(g_k[:MLA_NOPE]), col(g_k[MLA_NOPE:]),
      inv_freq.reshape(-1, 1))


def _attn_kernel(qt_ref, k_ref, vt_ref, o_ref, m_ref, l_ref, acc_ref, kbuf_ref, vbuf_ref, *, running_max):
    qi = pl.program_id(1)
    m_ref[...] = jnp.full(m_ref.shape, -jnp.inf, F32)
    l_ref[...] = jnp.zeros(l_ref.shape, F32)
    acc_ref[...] = jnp.zeros(acc_ref.shape, F32)
    kbuf_ref[qi] = k_ref[0]
    vbuf_ref[qi] = vt_ref[0, 0]

    units = [(hd, part) for hd in range(MLA_HEADS) for part in range(TK // KSUB)]

    def scores(j, unit, t0):
        hd, part = unit
        k = kbuf_ref[j, hd, part * KSUB:(part + 1) * KSUB, :]
        return _dot(k, qt_ref[0, hd, :, t0:])

    def kv_tile(j, masked):
        first_q = lambda unit: unit[1] * KSUB if masked else 0
        pending = [scores(j, u, first_q(u)) for u in units[:LOOKAHEAD]]
        for i, (hd, part) in enumerate(units):
            s = pending.pop(0)
            if i + LOOKAHEAD < len(units):
                nxt = units[i + LOOKAHEAD]
                pending.append(scores(j, nxt, first_q(nxt)))
            t0 = first_q((hd, part))
            if masked:
                row_s = lax.broadcasted_iota(jnp.int32, s.shape, 0) + part * KSUB
                col_t = lax.broadcasted_iota(jnp.int32, s.shape, 1) + t0
                s = jnp.where(row_s <= col_t, s, -jnp.inf)
            v_t = vbuf_ref[j, hd, :, part * KSUB:(part + 1) * KSUB]
            if running_max:
                m_old = m_ref[hd, :, t0:]
                m_new = jnp.maximum(m_old, jnp.max(s, axis=0, keepdims=True))
                alpha = jnp.exp2(m_old - m_new)
                p = jnp.exp2(s - m_new)
                m_ref[hd, :, t0:] = m_new
                l_ref[hd, :, t0:] = alpha * l_ref[hd, :, t0:] + jnp.sum(p, axis=0, keepdims=True)
                acc_ref[hd, :, t0:] = alpha * acc_ref[hd, :, t0:] + _dot(v_t, p.astype(BF16))
            else:
                p = jnp.exp2(s)
                l_ref[hd, :, t0:] += jnp.sum(p, axis=0, keepdims=True)
                acc_ref[hd, :, t0:] += _dot(v_t, p.astype(BF16))

    def body(j, _):
        kv_tile(j, False)
        return 0

    lax.fori_loop(0, qi, body, 0)
    kv_tile(qi, True)
    out_t = jnp.concatenate([acc_ref[hd] / l_ref[hd] for hd in range(MLA_HEADS)], axis=0)
    o_ref[0] = out_t.T.astype(BF16)


def _attention(qt, k, vt, *, running_max):
    bsz, _, _, seq = qt.shape
    return pl.pallas_call(
        functools.partial(_attn_kernel, running_max=running_max),
        grid=(bsz, seq // TQ),
        in_specs=[pl.BlockSpec((1, MLA_HEADS, HEAD_PAD, TQ), lambda b, i: (b, 0, 0, i)),
                  pl.BlockSpec((1, MLA_HEADS, TK, HEAD_PAD), lambda b, i: (b, 0, i, 0)),
                  pl.BlockSpec((1, 1, MLA_HEADS, MLA_V, TK), lambda b, i: (b, i, 0, 0, 0))],
        out_specs=pl.BlockSpec((1, TQ, MLA_HEADS * MLA_V), lambda b, i: (b, i, 0)),
        out_shape=jax.ShapeDtypeStruct((bsz, seq, MLA_HEADS * MLA_V), BF16),
        scratch_shapes=[pltpu.VMEM((MLA_HEADS, 1, TQ), F32), pltpu.VMEM((MLA_HEADS, 1, TQ), F32),
                        pltpu.VMEM((MLA_HEADS, MLA_V, TQ), F32),
                        pltpu.VMEM((seq // TK, MLA_HEADS, TK, HEAD_PAD), BF16),
                        pltpu.VMEM((seq // TK, MLA_HEADS, MLA_V, TK), BF16)],
        compiler_params=_params(2),
        name="attention_running_max" if running_max else "attention",
    )(qt, k, vt)


def _split3(x):
    hi = x.astype(BF16)
    r = x - hi.astype(F32)
    mid = r.astype(BF16)
    lo = (r - mid.astype(F32)).astype(BF16)
    return hi, mid, lo


def _mlstm_kernel(mqt_ref, mk_ref, mvt_ref, mo_ref, g_ref, gt_ref, gmh_ref, yd_ref, ct_ref, m_ref):
    bsz = mqt_ref.shape[0]

    @pl.when(pl.program_id(0) == 0)
    def _():
        ct_ref[...] = jnp.zeros(ct_ref.shape, F32)
        m_ref[...] = jnp.zeros(m_ref.shape, F32)

    row_s = lax.broadcasted_iota(jnp.int32, (ML, ML), 0)
    col_t = lax.broadcasted_iota(jnp.int32, (ML, ML), 1)
    visible = row_s <= col_t
    upper = jnp.where(visible, 1.0, 0.0).astype(BF16)
    lower = jnp.where(col_t <= row_s, 1.0, 0.0).astype(BF16)
    lane = lax.broadcasted_iota(jnp.int32, (ML, LANES), 1)
    k_lo = lane < MLSTM_QK
    ones_rows = jnp.where(lax.broadcasted_iota(jnp.int32, (V_EXT - MLSTM_V, ML), 0) == 0, 1.0, 0.0)
    i_lane0 = MLA_ROPE

    chains = [(b, hd) for b in range(bsz) for hd in range(MLSTM_HEADS)]
    a_row, i_row, c_col = {}, {}, {}
    for b in range(bsz):
        gcol = g_ref[b]
        grow = gt_ref[b]
        f_al = pltpu.roll(gcol, LANES - MLSTM_HEADS, axis=1)
        a_al = sum(_dot(lower, piece) for piece in _split3(f_al))
        a_rows = sum(_dot(piece, upper) for piece in _split3(jnp.concatenate([grow, grow], axis=0)))
        c_all = gcol - a_al
        for hd in range(MLSTM_HEADS):
            a_row[b, hd] = a_rows[MLSTM_HEADS + hd:MLSTM_HEADS + hd + 1]
            i_row[b, hd] = grow[hd:hd + 1]
            c_col[b, hd] = c_all[:, i_lane0 + hd:i_lane0 + hd + 1]

    k_m, qt_pair, vt_ext, sqk, qc = {}, {}, {}, {}, {}
    for b, hd in chains:
        pair, half = hd // 2, hd % 2
        k_pair = mk_ref[b, :, pair * LANES:(pair + 1) * LANES]
        k_m[b, hd] = jnp.where(k_lo if half == 0 else ~k_lo, k_pair.astype(F32), 0.0).astype(BF16)
        qt_pair[b, hd] = mqt_ref[b, pair * LANES:(pair + 1) * LANES, :]
        sqk[b, hd] = _dot(k_m[b, hd], qt_pair[b, hd])
    for b, hd in chains:
        idx = b * MLSTM_HEADS + hd
        qc[b, hd] = _dot(ct_ref[idx].astype(BF16), qt_pair[b, hd])

    m_t, m_new, m_prev, w_inter, s_t, den_intra = {}, {}, {}, {}, {}, {}
    for b, hd in chains:
        idx = b * MLSTM_HEADS + hd
        m_prev[b, hd] = m_ref[idx][0:1, 0:1]
        d_t = jnp.where(visible, a_row[b, hd] + c_col[b, hd], -jnp.inf)
        inter = a_row[b, hd] + m_prev[b, hd]
        m_t[b, hd] = jnp.maximum(inter, jnp.max(d_t, axis=0, keepdims=True))
        w_inter[b, hd] = jnp.exp(inter - m_t[b, hd])
        s_t[b, hd] = sqk[b, hd] * jnp.exp(d_t - m_t[b, hd])
        den_intra[b, hd] = jnp.sum(s_t[b, hd], axis=0, keepdims=True)

    nv = {}
    for b, hd in chains:
        v_t = mvt_ref[b, hd * MLSTM_V:(hd + 1) * MLSTM_V, :]
        vt_ext[b, hd] = jnp.concatenate([v_t.astype(F32), ones_rows], axis=0)
        nv[b, hd] = _dot(vt_ext[b, hd].astype(BF16), s_t[b, hd].astype(BF16))

    for b, hd in chains:
        idx = b * MLSTM_HEADS + hd
        a_last = a_row[b, hd][:, ML - 1:ML]
        m_new[b, hd] = m_t[b, hd][:, ML - 1:ML]
        w_s = jnp.exp(a_last - a_row[b, hd] + i_row[b, hd] - m_new[b, hd])
        decay = jnp.exp(a_last + m_prev[b, hd] - m_new[b, hd])
        upd = _dot((vt_ext[b, hd] * w_s).astype(BF16), k_m[b, hd])
        ct_ref[idx] = decay * ct_ref[idx] + upd
        m_ref[idx] = jnp.broadcast_to(m_new[b, hd], (8, LANES))

    for b in range(bsz):
        h_t = []
        for hd in range(MLSTM_HEADS):
            num = nv[b, hd][:MLSTM_V] + w_inter[b, hd] * qc[b, hd][:MLSTM_V]
            den = den_intra[b, hd] + w_inter[b, hd] * qc[b, hd][MLSTM_V:MLSTM_V + 1]
            hh = num / jnp.maximum(jnp.abs(den), jnp.exp(-m_t[b, hd]))
            h_t.append(hh * lax.rsqrt(jnp.mean(hh * hh, axis=0, keepdims=True) + EPS))
        hn = jnp.concatenate(h_t, axis=0).T * gmh_ref[...]
        yd_ref[b] = (_sigmoid(mo_ref[b]) * hn).astype(BF16)


def _mlstm(mqt, mk, mvt, mo, g, gt, g_mh):
    bsz, seq, _ = mk.shape
    tile3 = lambda w: pl.BlockSpec((bsz, ML, w), lambda i: (0, i, 0))
    tile3t = lambda r: pl.BlockSpec((bsz, r, ML), lambda i: (0, 0, i))
    return pl.pallas_call(
        _mlstm_kernel,
        grid=(seq // ML,),
        in_specs=[tile3t(256), tile3(256), tile3t(512), tile3(512),
                  tile3(LANES), tile3t(2 * MLSTM_HEADS), _const_spec((1, MLSTM_HEADS * MLSTM_V))],
        out_specs=tile3(512),
        out_shape=jax.ShapeDtypeStruct((bsz, seq, MLSTM_HEADS * MLSTM_V), BF16),
        scratch_shapes=[pltpu.VMEM((bsz * MLSTM_HEADS, V_EXT, 2 * MLSTM_QK), F32),
                        pltpu.VMEM((bsz * MLSTM_HEADS, 8, LANES), F32)],
        compiler_params=_params(1),
        name="mlstm",
    )(mqt, mk, mvt, mo, g, gt, g_mh.reshape(1, MLSTM_HEADS * MLSTM_V))


def kernel(x, p, positions, g_mix, g_ffn, g_ple, ev_w_in, ev_w_conv, ev_g_v, ev_w_s, ev_b_s, ev_w_out,
           od_w_in, od_b_gate, od_g_qa, od_g_kva, od_w_q_up, od_w_kv_up, od_g_q, od_g_k, od_g_mh, od_w_out,
           w_gate, w_up, w_down, w_ple_proj, w_ple_gate):
    h = x
    depth = g_mix.shape[0]
    ffn_w = tuple(w.astype(BF16) for w in (w_gate, w_up, w_down, w_ple_gate, w_ple_proj))
    for layer in range(depth):
        j = layer // 2
        if layer % 2 == 0:
            y1, y2 = _even_mixer(h, g_mix[layer], ev_w_in[j], ev_w_conv[j], ev_g_v[j], ev_w_s[j], ev_b_s[j])
            w_out = ev_w_out[j]
        else:
            qt, k, vt, spread, mqt, mk, mvt, mo, g, gt = _odd_proj(
                h, positions, g_mix[layer], od_w_in[j], od_b_gate[j], od_g_qa[j], od_g_kva[j],
                od_w_q_up[j], od_w_kv_up[j], od_g_q[j], od_g_k[j])
            y1 = lax.cond(jnp.max(spread) <= MAX_SCORE_SPREAD,
                          functools.partial(_attention, running_max=False),
                          functools.partial(_attention, running_max=True), qt, k, vt)
            y2 = _mlstm(mqt, mk, mvt, mo, g, gt, od_g_mh[j])
            w_out = od_w_out[j]
        h = _ffn_ple(layer, h, y1, y2, p, w_out, g_ffn[layer], g_ple[layer], *ffn_w)
    return h
```

```python
import functools
import math

import numpy as np
import jax
import jax.numpy as jnp
from jax import lax
from jax.experimental import pallas as pl
from jax.experimental.pallas import tpu as pltpu

F32 = jnp.float32
BF16 = jnp.bfloat16

D_MODEL = 1024
D_PLE = 256
EPS = 1e-6

CONV_DIM = 512
GMLP_DIM = 512
GMLP_HEADS = 8
GMLP_HEAD_DIM = 64
GMLP_CHUNK = 128
EVEN_IN = 3 * CONV_DIM + 2 * GMLP_DIM

MLA_HEADS = 8
MLA_NOPE = 64
MLA_ROPE = 32
MLA_V = 64
MLA_QK = MLA_NOPE + MLA_ROPE
Q_LORA = 384
KV_LORA = 256
ROPE_THETA = 10000.0
MLSTM_HEADS = 4
MLSTM_QK = 64
MLSTM_V = 128
D_FF = 2816

LANES = 128
MXU_EDGE = 256

TM = 512
TM_PROJ = 1024
TM_ODD = 512
TM_FFN = 512
TQ = 512
TK = 512
KSUB = 256
LOOKAHEAD = 2
ML = 256
V_EXT = MLSTM_V + 16
HEAD_PAD = 128
ODD_IN_PAD = 2304
TAIL_OFF = ODD_IN_PAD - LANES
FF_CHUNKS = ((0, 1024), (1024, 2048), (2048, 2816))
MAX_SCORE_SPREAD = 200.0
SCORE_SCALE = MLA_QK ** -0.5 * math.log2(math.e)
BF16_SLACK = 1.01

VMEM_LIMIT = 56 * 1024 * 1024


def _dot(a, b):
    return jnp.dot(a, b, preferred_element_type=F32)


def _dot_nt(a, b):
    return lax.dot_general(a, b, (((1,), (1,)), ((), ())), preferred_element_type=F32)


def _rms_rows(x, g):
    ms = jnp.mean(x * x, axis=-1, keepdims=True)
    return x * lax.rsqrt(ms + EPS) * g


def _rms_cols(x, g):
    ms = jnp.mean(x * x, axis=0, keepdims=True)
    return x * lax.rsqrt(ms + EPS) * g


def _sigmoid(x):
    return 0.5 * jnp.tanh(0.5 * x) + 0.5


def _const_spec(shape):
    zeros = (0,) * len(shape)
    return pl.BlockSpec(shape, lambda *_: zeros, pipeline_mode=pl.Buffered(1))


def _params(n_axes):
    return pltpu.CompilerParams(dimension_semantics=("arbitrary",) * n_axes,
                                vmem_limit_bytes=VMEM_LIMIT)


def _even_kernel(h_ref, gmix_ref, win_ref, wconv_ref, gmat_ref, gv_ref, wpair_ref, bm_ref, *rest, n_cast):
    cast_in, cast_out = rest[:n_cast], rest[n_cast + 2:2 * n_cast + 2]
    ya_ref, yb_ref = rest[n_cast:n_cast + 2]
    zbuf_ref = rest[-1]
    for src_ref, dst_ref in zip(cast_in, cast_out):
        dst_ref[...] = src_ref[...].astype(BF16)

    @pl.when(pl.program_id(1) == 0)
    def _():
        zbuf_ref[0:8, :] = jnp.zeros((8, CONV_DIM), F32)

    lane = lax.broadcasted_iota(jnp.int32, (GMLP_CHUNK, LANES), 1)
    row_t = lax.broadcasted_iota(jnp.int32, (GMLP_CHUNK, 2 * GMLP_CHUNK), 0)
    col_s = lax.broadcasted_iota(jnp.int32, (GMLP_CHUNK, 2 * GMLP_CHUNK), 1) % GMLP_CHUNK
    tril = col_s <= row_t
    wms = [jnp.where(tril, wpair_ref[j], 0.0).astype(BF16) for j in range(GMLP_HEADS // 2)]
    col = lambda i: win_ref[:, i * CONV_DIM:(i + 1) * CONV_DIM]
    wc = wconv_ref[...]

    for s0 in range(0, TM_PROJ, TM):
        hn = _rms_rows(h_ref[0, s0:s0 + TM], gmix_ref[...]).astype(BF16)
        v = _dot(hn, col(4))
        u = _dot(hn, col(3))
        z_cx = _dot(hn, win_ref[:, CONV_DIM:3 * CONV_DIM])
        gv = jax.nn.gelu(v)
        ss = _dot((gv * gv).astype(BF16), gmat_ref[...])
        b_gate = _dot(hn, col(0))

        zz = z_cx[:, 0:CONV_DIM] * z_cx[:, CONV_DIM:2 * CONV_DIM]
        zbuf_ref[8 + s0:8 + s0 + TM, :] = zz
        z1 = zbuf_ref[7 + s0:7 + s0 + TM, :]
        z2 = zbuf_ref[6 + s0:6 + s0 + TM, :]
        conv = wc[2:3] * zz + wc[1:2] * z1 + wc[0:1] * z2
        ya_ref[0, s0:s0 + TM] = (b_gate * conv).astype(BF16)
        if s0 + TM == TM_PROJ:
            zbuf_ref[0:8, :] = zz[TM - 8:TM, :]

        gu = jax.nn.gelu(u)
        vn = gv * lax.rsqrt(ss * (1.0 / GMLP_HEAD_DIM) + EPS) * gv_ref[...]
        for c in range(TM // GMLP_CHUNK):
            r0 = c * GMLP_CHUNK
            outs = []
            for j in range(GMLP_HEADS // 2):
                vp = vn[r0:r0 + GMLP_CHUNK, j * LANES:(j + 1) * LANES]
                lo = jnp.where(lane < GMLP_HEAD_DIM, vp, 0.0).astype(BF16)
                hi = jnp.where(lane >= GMLP_HEAD_DIM, vp, 0.0).astype(BF16)
                outs.append(_dot(wms[j], jnp.concatenate([lo, hi], axis=0)))
            mixed = jnp.concatenate(outs, axis=1) + bm_ref[...]
            yb_ref[0, s0 + r0:s0 + r0 + GMLP_CHUNK, :] = (gu[r0:r0 + GMLP_CHUNK] * mixed).astype(BF16)


def _even_mixer(h, g_mix, w_in, w_conv, g_v, w_s, b_s, stacked_weights):
    bsz, seq, _ = h.shape
    steps = bsz * (seq // TM_PROJ)
    slab = lambda w: pl.BlockSpec((w.shape[0], w.shape[1] // steps, w.shape[2]),
                                  lambda b, i: (0, b * (seq // TM_PROJ) + i, 0))
    for w in stacked_weights:
        assert w.shape[1] % (16 * steps) == 0, "slab rows must fill bf16 (16, 128) tiles"
    gmat = jnp.asarray(np.kron(np.eye(GMLP_HEADS), np.ones((GMLP_HEAD_DIM, GMLP_HEAD_DIM))), BF16)
    wpair = w_s.reshape(GMLP_HEADS // 2, 2, GMLP_CHUNK, GMLP_CHUNK).transpose(0, 2, 1, 3)
    wpair = wpair.reshape(GMLP_HEADS // 2, GMLP_CHUNK, 2 * GMLP_CHUNK)
    bm = jnp.repeat(b_s.T, GMLP_HEAD_DIM, axis=1)
    out_sds = jax.ShapeDtypeStruct((bsz, seq, CONV_DIM), BF16)
    tile = lambda w: pl.BlockSpec((1, TM_PROJ, w), lambda b, i: (b, i, 0))
    outs = pl.pallas_call(
        functools.partial(_even_kernel, n_cast=len(stacked_weights)),
        grid=(bsz, seq // TM_PROJ),
        in_specs=[tile(D_MODEL), _const_spec((1, D_MODEL)), _const_spec((D_MODEL, EVEN_IN)),
                  _const_spec((3, CONV_DIM)), _const_spec((GMLP_DIM, GMLP_DIM)), _const_spec((1, GMLP_DIM)),
                  _const_spec((GMLP_HEADS // 2, GMLP_CHUNK, 2 * GMLP_CHUNK)),
                  _const_spec((GMLP_CHUNK, GMLP_DIM))] + [slab(w) for w in stacked_weights],
        out_specs=[tile(CONV_DIM), tile(GMLP_DIM)] + [slab(w) for w in stacked_weights],
        out_shape=[out_sds, out_sds] + [jax.ShapeDtypeStruct(w.shape, BF16) for w in stacked_weights],
        scratch_shapes=[pltpu.VMEM((TM_PROJ + 8, CONV_DIM), F32)],
        compiler_params=_params(2),
        name="even_mixer",
    )(h, g_mix.reshape(1, D_MODEL), w_in.astype(BF16), w_conv, gmat, g_v.reshape(1, GMLP_DIM), wpair, bm,
      *stacked_weights)
    return outs[0], outs[1], tuple(outs[2:])


def _ffn_kernel(h_ref, y1_ref, y2_ref, p_ref, wout_ref, gffn_ref, wg_ref, wu_ref, wd_ref,
                gple_ref, wpg_ref, wpp_ref, o_ref):
    rows = [slice(r, r + TM_FFN // 2) for r in (0, TM_FFN // 2)]
    p_b = p_ref[0, 0].astype(BF16)
    half = D_MODEL // 2
    xs = [h_ref[0, r] + _dot(jnp.concatenate([y1_ref[0, r], y2_ref[0, r]], axis=-1), wout_ref[...])
          for r in rows]
    pp_lo = _dot(p_b, wpp_ref[:, :half])
    x = jnp.concatenate(xs, axis=0)
    hn = jnp.concatenate([_rms_rows(xr, gffn_ref[...]).astype(BF16) for xr in xs], axis=0)
    acc = x
    for lo, hi in FF_CHUNKS:
        g = _dot(hn, wg_ref[:, lo:hi])
        u = _dot(hn, wu_ref[:, lo:hi])
        a = (g * _sigmoid(g) * u).astype(BF16)
        acc = acc + _dot(a, wd_ref[lo:hi, :])
    pp_hi = _dot(p_b, wpp_ref[:, half:])
    pp = jnp.concatenate([pp_lo, pp_hi], axis=-1)
    for r in rows:
        hn2 = _rms_rows(acc[r], gple_ref[...]).astype(BF16)
        o_ref[0, r] = acc[r] + _sigmoid(_dot(hn2, wpg_ref[...])) * pp[r]


def _ffn_ple(layer, h, y1, y2, p, w_out, g_ffn, g_ple, w_gate, w_up, w_down, w_pg, w_pp):
    bsz, seq, _ = h.shape
    tile = lambda w: pl.BlockSpec((1, TM_FFN, w), lambda b, i: (b, i, 0))
    layer_spec = lambda r, c: pl.BlockSpec((None, r, c), lambda *_: (layer, 0, 0), pipeline_mode=pl.Buffered(1))
    return pl.pallas_call(
        _ffn_kernel,
        grid=(bsz, seq // TM_FFN),
        in_specs=[tile(D_MODEL), tile(512), tile(512),
                  pl.BlockSpec((1, 1, TM_FFN, D_PLE), lambda b, i: (layer, b, i, 0)),
                  _const_spec((D_MODEL, D_MODEL)), _const_spec((1, D_MODEL)),
                  layer_spec(D_MODEL, D_FF), layer_spec(D_MODEL, D_FF), layer_spec(D_FF, D_MODEL),
                  _const_spec((1, D_MODEL)), layer_spec(D_MODEL, D_MODEL), layer_spec(D_PLE, D_MODEL)],
        out_specs=tile(D_MODEL),
        out_shape=jax.ShapeDtypeStruct((bsz, seq, D_MODEL), F32),
        compiler_params=_params(2),
        name="ffn_ple",
    )(h, y1, y2, p, w_out.astype(BF16), g_ffn.reshape(1, D_MODEL), w_gate, w_up, w_down,
      g_ple.reshape(1, D_MODEL), w_pg, w_pp)


def _log_sigmoid(x):
    return jnp.minimum(x, 0.0) - jnp.log1p(jnp.exp(-jnp.abs(x)))


def _rope_cols(xr, cos, sin):
    half = MLA_ROPE // 2
    x1, x2 = xr[:half], xr[half:]
    return x1 * cos - x2 * sin, x2 * cos + x1 * sin


def _join_kernel(wlat_ref, wml_ref, wtail_ref, win_ref):
    win_ref[:, 0:Q_LORA + KV_LORA] = wlat_ref[...]
    win_ref[:, Q_LORA + KV_LORA:TAIL_OFF] = wml_ref[...]
    win_ref[:, TAIL_OFF:ODD_IN_PAD] = wtail_ref[...]


def _odd_kernel(h_ref, pos_ref, gmix_ref, win_ref, bias_ref, gqa_ref, gkva_ref, wqt_ref,
                wkvt_ref, gqn_ref, gqr_ref, gkn_ref, gkr_ref, freq_ref,
                qt_ref, k_ref, vt_ref, spread_ref, mqt_ref, mk_ref, mvt_ref, mo_ref, g_ref, gt_ref):
    for sub in range(TM_ODD // TM):
        _odd_sub_tile(sub, h_ref, pos_ref, gmix_ref, win_ref, bias_ref, gqa_ref, gkva_ref, wqt_ref, wkvt_ref,
                      gqn_ref, gqr_ref, gkn_ref, gkr_ref, freq_ref,
                      qt_ref, k_ref, vt_ref, spread_ref, mqt_ref, mk_ref, mvt_ref, mo_ref, g_ref, gt_ref)


def _odd_sub_tile(sub, h_ref, pos_ref, gmix_ref, win_ref, bias_ref, gqa_ref, gkva_ref, wqt_ref, wkvt_ref,
                  gqn_ref, gqr_ref, gkn_ref, gkr_ref, freq_ref,
                  qt_ref, k_ref, vt_ref, spread_ref, mqt_ref, mk_ref, mvt_ref, mo_ref, g_ref, gt_ref):
    tok = slice(sub * TM, (sub + 1) * TM)
    hn = _rms_rows(h_ref[0, tok], gmix_ref[...]).astype(BF16)
    z = _dot(hn, win_ref[...])
    q_lat = z[:, 0:384]
    kv_lat = z[:, 384:640]

    mqt_ref[0, :, tok] = (z[:, 640:896] * (MLSTM_QK ** -0.5)).T.astype(BF16)
    mk_ref[0, tok] = z[:, 896:1152].astype(BF16)
    mvt_ref[0, :, tok] = z[:, 1152:1664].T.astype(BF16)
    mo_ref[0, tok] = z[:, 1664:2176]
    tail = z[:, TAIL_OFF:ODD_IN_PAD] + bias_ref[...]
    lane = lax.broadcasted_iota(jnp.int32, (TM, LANES), 1)
    f_lane = (lane >= MLA_ROPE + MLSTM_HEADS) & (lane < MLA_ROPE + 2 * MLSTM_HEADS)
    tail = jnp.where(f_lane, _log_sigmoid(tail), tail)
    g_ref[0, tok] = tail
    tail_t = tail.T
    gt_ref[0, :, tok] = tail_t[MLA_ROPE:MLA_ROPE + 2 * MLSTM_HEADS]

    qn = _rms_rows(q_lat, gqa_ref[...]).astype(BF16)
    kvn = _rms_rows(kv_lat, gkva_ref[...]).astype(BF16)
    q_t = _dot_nt(wqt_ref[...], qn)
    kv_t = _dot_nt(wkvt_ref[...], kvn)

    ang = freq_ref[...] * pos_ref[0, :, tok].astype(F32)
    cos = jnp.cos(ang)
    sin = jnp.sin(ang)
    kr1, kr2 = _rope_cols(_rms_cols(tail_t[0:MLA_ROPE], gkr_ref[...]), cos, sin)

    norm_bound = lambda gn_ref, gr_ref: jnp.sqrt(
        MLA_NOPE * jnp.max(gn_ref[...] ** 2, axis=0, keepdims=True)
        + MLA_ROPE * jnp.max(gr_ref[...] ** 2, axis=0, keepdims=True))
    upper = BF16_SLACK * norm_bound(gqn_ref, gqr_ref) * norm_bound(gkn_ref, gkr_ref)
    first_row = lax.broadcasted_iota(jnp.int32, (8, TM), 0) == 0
    zeros_tail = jnp.zeros((HEAD_PAD - MLA_QK - 8, TM), F32)
    k_pad = jnp.concatenate([jnp.ones((8, TM), F32), zeros_tail], axis=0)
    spreads = []
    for hd in range(MLA_HEADS):
        q0 = hd * MLA_QK
        q_nope = _rms_cols(q_t[q0:q0 + MLA_NOPE], gqn_ref[...])
        qr1, qr2 = _rope_cols(_rms_cols(q_t[q0 + MLA_NOPE:q0 + MLA_QK], gqr_ref[...]), cos, sin)
        q96 = jnp.concatenate([q_nope, qr1, qr2], axis=0)
        k0 = hd * (MLA_NOPE + MLA_V)
        k_nope = _rms_cols(kv_t[k0:k0 + MLA_NOPE], gkn_ref[...])
        k96 = jnp.concatenate([k_nope, kr1, kr2], axis=0)
        lower = jnp.sum(q96 * k96, axis=0, keepdims=True)
        shift = 0.5 * (upper + lower)
        spreads.append(upper - lower)
        q_full = jnp.concatenate([q96, jnp.where(first_row, -shift, 0.0), zeros_tail], axis=0)
        qt_ref[0, hd, :, tok] = q_full.astype(BF16)
        k_ref[0, hd, tok] = jnp.concatenate([k96, k_pad], axis=0).T.astype(BF16)
        vt_ref[0, sub, hd] = kv_t[k0 + MLA_NOPE:k0 + MLA_NOPE + MLA_V].astype(BF16)
    spread_ref[0, :, tok] = jnp.concatenate(spreads, axis=0)


def _odd_proj(h, positions, g_mix, w_in, b_gate, g_qa, g_kva, w_q_up, w_kv_up, g_q, g_k):
    bsz, seq, _ = h.shape
    c = np.cumsum([0, Q_LORA + KV_LORA, MLA_ROPE, 2 * MLSTM_HEADS * (MLSTM_QK + MLSTM_V), 2 * MLSTM_HEADS])
    seg = lambda i: w_in[:, c[i]:c[i + 1]].astype(BF16)
    tail_pad = jnp.zeros((D_MODEL, LANES - MLA_ROPE - 2 * MLSTM_HEADS), BF16)
    groups = (seg(0), seg(2), jnp.concatenate([seg(1), seg(3), tail_pad], axis=1))
    w_in_r = pl.pallas_call(
        _join_kernel,
        out_shape=jax.ShapeDtypeStruct((D_MODEL, ODD_IN_PAD), BF16),
        compiler_params=pltpu.CompilerParams(vmem_limit_bytes=VMEM_LIMIT),
        name="odd_w_in_join",
    )(*groups)
    bias = jnp.zeros((1, LANES), F32).at[0, MLA_ROPE:MLA_ROPE + 2 * MLSTM_HEADS].set(b_gate)
    inv_freq = ROPE_THETA ** (-jnp.arange(0, MLA_ROPE, 2, dtype=F32) / MLA_ROPE)
    col = lambda a: a.reshape(-1, 1)
    tile3 = lambda w: pl.BlockSpec((1, TM_ODD, w), lambda b, i: (b, i, 0))
    tile3t = lambda r: pl.BlockSpec((1, r, TM_ODD), lambda b, i: (b, 0, i))
    out_shape = [
        jax.ShapeDtypeStruct((bsz, MLA_HEADS, HEAD_PAD, seq), BF16),
        jax.ShapeDtypeStruct((bsz, MLA_HEADS, seq, HEAD_PAD), BF16),
        jax.ShapeDtypeStruct((bsz, seq // TK, MLA_HEADS, MLA_V, TK), BF16),
        jax.ShapeDtypeStruct((bsz, MLA_HEADS, seq), F32),
        jax.ShapeDtypeStruct((bsz, 256, seq), BF16),
        jax.ShapeDtypeStruct((bsz, seq, 256), BF16),
        jax.ShapeDtypeStruct((bsz, 512, seq), BF16),
        jax.ShapeDtypeStruct((bsz, seq, 512), F32),
        jax.ShapeDtypeStruct((bsz, seq, LANES), F32),
        jax.ShapeDtypeStruct((bsz, 2 * MLSTM_HEADS, seq), F32),
    ]
    out_specs = [
        pl.BlockSpec((1, MLA_HEADS, HEAD_PAD, TM_ODD), lambda b, i: (b, 0, 0, i)),
        pl.BlockSpec((1, MLA_HEADS, TM_ODD, HEAD_PAD), lambda b, i: (b, 0, i, 0)),
        pl.BlockSpec((1, TM_ODD // TK, MLA_HEADS, MLA_V, TK), lambda b, i: (b, i, 0, 0, 0)),
        tile3t(MLA_HEADS), tile3t(256), tile3(256), tile3t(512), tile3(512), tile3(LANES),
        tile3t(2 * MLSTM_HEADS),
    ]
    assert TM == TK, "one v^T key tile is written per sub-tile"
    return pl.pallas_call(
        _odd_kernel,
        grid=(bsz, seq // TM_ODD),
        in_specs=[tile3(D_MODEL), tile3t(1),
                  _const_spec((1, D_MODEL)), _const_spec((D_MODEL, ODD_IN_PAD)), _const_spec((1, LANES)),
                  _const_spec((1, Q_LORA)), _const_spec((1, KV_LORA)),
                  _const_spec((MLA_HEADS * MLA_QK, Q_LORA)), _const_spec((MLA_HEADS * (MLA_NOPE + MLA_V), KV_LORA)),
                  _const_spec((MLA_NOPE, 1)), _const_spec((MLA_ROPE, 1)),
                  _const_spec((MLA_NOPE, 1)), _const_spec((MLA_ROPE, 1)), _const_spec((MLA_ROPE // 2, 1))],
        out_specs=out_specs,
        out_shape=out_shape,
        compiler_params=_params(2),
        name="odd_proj",
    )(h, positions.reshape(bsz, 1, seq), g_mix.reshape(1, D_MODEL), w_in_r, bias,
      g_qa.reshape(1, Q_LORA), g_kva.reshape(1, KV_LORA), w_q_up.T.astype(BF16), w_kv_up.T.astype(BF16),
      col(g_q[:MLA_NOPE] * SCORE_SCALE), col(g_q[MLA_NOPE:] * SCORE_SCALE),
      col(g_k[:MLA_NOPE]), col(g_k[MLA_NOPE:]),
      inv_freq.reshape(-1, 1))


def _attn_kernel(qt_ref, k_ref, vt_ref, o_ref, m_ref, l_ref, acc_ref, kbuf_ref, vbuf_ref, *, running_max):
    qi = pl.program_id(1)
    m_ref[...] = jnp.full(m_ref.shape, -jnp.inf, F32)
    l_ref[...] = jnp.zeros(l_ref.shape, F32)
    acc_ref[...] = jnp.zeros(acc_ref.shape, F32)
    kbuf_ref[qi] = k_ref[0]
    vbuf_ref[qi] = vt_ref[0, 0]

    units = [(hd, part) for hd in range(MLA_HEADS) for part in range(TK // KSUB)]

    def scores(j, unit, t0):
        hd, part = unit
        k = kbuf_ref[j, hd, part * KSUB:(part + 1) * KSUB, :]
        return _dot(k, qt_ref[0, hd, :, t0:])

    def kv_tile(j, masked):
        first_q = lambda unit: unit[1] * KSUB if masked else 0
        pending = [scores(j, u, first_q(u)) for u in units[:LOOKAHEAD]]
        for i, (hd, part) in enumerate(units):
            s = pending.pop(0)
            if i + LOOKAHEAD < len(units):
                nxt = units[i + LOOKAHEAD]
                pending.append(scores(j, nxt, first_q(nxt)))
            t0 = first_q((hd, part))
            if masked:
                row_s = lax.broadcasted_iota(jnp.int32, s.shape, 0) + part * KSUB
                col_t = lax.broadcasted_iota(jnp.int32, s.shape, 1) + t0
                s = jnp.where(row_s <= col_t, s, -jnp.inf)
            v_t = vbuf_ref[j, hd, :, part * KSUB:(part + 1) * KSUB]
            if running_max:
                m_old = m_ref[hd, :, t0:]
                m_new = jnp.maximum(m_old, jnp.max(s, axis=0, keepdims=True))
                alpha = jnp.exp2(m_old - m_new)
                p = jnp.exp2(s - m_new)
                m_ref[hd, :, t0:] = m_new
                l_ref[hd, :, t0:] = alpha * l_ref[hd, :, t0:] + jnp.sum(p, axis=0, keepdims=True)
                acc_ref[hd, :, t0:] = alpha * acc_ref[hd, :, t0:] + _dot(v_t, p.astype(BF16))
            else:
                p = jnp.exp2(s)
                l_ref[hd, :, t0:] += jnp.sum(p, axis=0, keepdims=True)
                acc_ref[hd, :, t0:] += _dot(v_t, p.astype(BF16))

    def body(j, _):
        kv_tile(j, False)
        return 0

    lax.fori_loop(0, qi, body, 0)
    kv_tile(qi, True)
    out_t = jnp.concatenate([acc_ref[hd] / l_ref[hd] for hd in range(MLA_HEADS)], axis=0)
    o_ref[0] = out_t.T.astype(BF16)


def _attention(qt, k, vt, *, running_max):
    bsz, _, _, seq = qt.shape
    return pl.pallas_call(
        functools.partial(_attn_kernel, running_max=running_max),
        grid=(bsz, seq // TQ),
        in_specs=[pl.BlockSpec((1, MLA_HEADS, HEAD_PAD, TQ), lambda b, i: (b, 0, 0, i)),
                  pl.BlockSpec((1, MLA_HEADS, TK, HEAD_PAD), lambda b, i: (b, 0, i, 0)),
                  pl.BlockSpec((1, 1, MLA_HEADS, MLA_V, TK), lambda b, i: (b, i, 0, 0, 0))],
        out_specs=pl.BlockSpec((1, TQ, MLA_HEADS * MLA_V), lambda b, i: (b, i, 0)),
        out_shape=jax.ShapeDtypeStruct((bsz, seq, MLA_HEADS * MLA_V), BF16),
        scratch_shapes=[pltpu.VMEM((MLA_HEADS, 1, TQ), F32), pltpu.VMEM((MLA_HEADS, 1, TQ), F32),
                        pltpu.VMEM((MLA_HEADS, MLA_V, TQ), F32),
                        pltpu.VMEM((seq // TK, MLA_HEADS, TK, HEAD_PAD), BF16),
                        pltpu.VMEM((seq // TK, MLA_HEADS, MLA_V, TK), BF16)],
        compiler_params=_params(2),
        name="attention_running_max" if running_max else "attention",
    )(qt, k, vt)


def _split3(x):
    hi = x.astype(BF16)
    r = x - hi.astype(F32)
    mid = r.astype(BF16)
    lo = (r - mid.astype(F32)).astype(BF16)
    return hi, mid, lo


def _mlstm_kernel(mqt_ref, mk_ref, mvt_ref, mo_ref, g_ref, gt_ref, gmh_ref, yd_ref, ct_ref, m_ref):
    bsz = mqt_ref.shape[0]

    @pl.when(pl.program_id(0) == 0)
    def _():
        ct_ref[...] = jnp.zeros(ct_ref.shape, F32)
        m_ref[...] = jnp.zeros(m_ref.shape, F32)

    row_s = lax.broadcasted_iota(jnp.int32, (ML, ML), 0)
    col_t = lax.broadcasted_iota(jnp.int32, (ML, ML), 1)
    visible = row_s <= col_t
    upper = jnp.where(visible, 1.0, 0.0).astype(BF16)
    lower = jnp.where(col_t <= row_s, 1.0, 0.0).astype(BF16)
    lane = lax.broadcasted_iota(jnp.int32, (ML, LANES), 1)
    k_lo = lane < MLSTM_QK
    ones_rows = jnp.where(lax.broadcasted_iota(jnp.int32, (V_EXT - MLSTM_V, ML), 0) == 0, 1.0, 0.0)
    i_lane0 = MLA_ROPE

    chains = [(b, hd) for b in range(bsz) for hd in range(MLSTM_HEADS)]
    a_row, i_row, c_col = {}, {}, {}
    for b in range(bsz):
        gcol = g_ref[b]
        grow = gt_ref[b]
        f_al = pltpu.roll(gcol, LANES - MLSTM_HEADS, axis=1)
        a_al = sum(_dot(lower, piece) for piece in _split3(f_al))
        a_rows = sum(_dot(piece, upper) for piece in _split3(jnp.concatenate([grow, grow], axis=0)))
        c_all = gcol - a_al
        for hd in range(MLSTM_HEADS):
            a_row[b, hd] = a_rows[MLSTM_HEADS + hd:MLSTM_HEADS + hd + 1]
            i_row[b, hd] = grow[hd:hd + 1]
            c_col[b, hd] = c_all[:, i_lane0 + hd:i_lane0 + hd + 1]

    k_m, qt_pair, vt_ext, sqk, qc = {}, {}, {}, {}, {}
    for b, hd in chains:
        pair, half = hd // 2, hd % 2
        k_pair = mk_ref[b, :, pair * LANES:(pair + 1) * LANES]
        k_m[b, hd] = jnp.where(k_lo if half == 0 else ~k_lo, k_pair.astype(F32), 0.0).astype(BF16)
        qt_pair[b, hd] = mqt_ref[b, pair * LANES:(pair + 1) * LANES, :]
        sqk[b, hd] = _dot(k_m[b, hd], qt_pair[b, hd])
    for b, hd in chains:
        idx = b * MLSTM_HEADS + hd
        qc[b, hd] = _dot(ct_ref[idx].astype(BF16), qt_pair[b, hd])

    m_t, m_new, m_prev, w_inter, s_t, den_intra = {}, {}, {}, {}, {}, {}
    for b, hd in chains:
        idx = b * MLSTM_HEADS + hd
        m_prev[b, hd] = m_ref[idx][0:1, 0:1]
        d_t = jnp.where(visible, a_row[b, hd] + c_col[b, hd], -jnp.inf)
        inter = a_row[b, hd] + m_prev[b, hd]
        m_t[b, hd] = jnp.maximum(inter, jnp.max(d_t, axis=0, keepdims=True))
        w_inter[b, hd] = jnp.exp(inter - m_t[b, hd])
        s_t[b, hd] = sqk[b, hd] * jnp.exp(d_t - m_t[b, hd])
        den_intra[b, hd] = jnp.sum(s_t[b, hd], axis=0, keepdims=True)

    nv = {}
    for b, hd in chains:
        v_t = mvt_ref[b, hd * MLSTM_V:(hd + 1) * MLSTM_V, :]
        vt_ext[b, hd] = jnp.concatenate([v_t.astype(F32), ones_rows], axis=0)
        nv[b, hd] = _dot(vt_ext[b, hd].astype(BF16), s_t[b, hd].astype(BF16))

    for b, hd in chains:
        idx = b * MLSTM_HEADS + hd
        a_last = a_row[b, hd][:, ML - 1:ML]
        m_new[b, hd] = m_t[b, hd][:, ML - 1:ML]
        w_s = jnp.exp(a_last - a_row[b, hd] + i_row[b, hd] - m_new[b, hd])
        decay = jnp.exp(a_last + m_prev[b, hd] - m_new[b, hd])
        upd = _dot((vt_ext[b, hd] * w_s).astype(BF16), k_m[b, hd])
        ct_ref[idx] = decay * ct_ref[idx] + upd
        m_ref[idx] = jnp.broadcast_to(m_new[b, hd], (8, LANES))

    for b in range(bsz):
        h_t = []
        for hd in range(MLSTM_HEADS):
            num = nv[b, hd][:MLSTM_V] + w_inter[b, hd] * qc[b, hd][:MLSTM_V]
            den = den_intra[b, hd] + w_inter[b, hd] * qc[b, hd][MLSTM_V:MLSTM_V + 1]
            hh = num / jnp.maximum(jnp.abs(den), jnp.exp(-m_t[b, hd]))
            h_t.append(hh * lax.rsqrt(jnp.mean(hh * hh, axis=0, keepdims=True) + EPS))
        hn = jnp.concatenate(h_t, axis=0).T * gmh_ref[...]
        yd_ref[b] = (_sigmoid(mo_ref[b]) * hn).astype(BF16)


def _mlstm(mqt, mk, mvt, mo, g, gt, g_mh):
    bsz, seq, _ = mk.shape
    tile3 = lambda w: pl.BlockSpec((bsz, ML, w), lambda i: (0, i, 0))
    tile3t = lambda r: pl.BlockSpec((bsz, r, ML), lambda i: (0, 0, i))
    return pl.pallas_call(
        _mlstm_kernel,
        grid=(seq // ML,),
        in_specs=[tile3t(256), tile3(256), tile3t(512), tile3(512),
                  tile3(LANES), tile3t(2 * MLSTM_HEADS), _const_spec((1, MLSTM_HEADS * MLSTM_V))],
        out_specs=tile3(512),
        out_shape=jax.ShapeDtypeStruct((bsz, seq, MLSTM_HEADS * MLSTM_V), BF16),
        scratch_shapes=[pltpu.VMEM((bsz * MLSTM_HEADS, V_EXT, 2 * MLSTM_QK), F32),
                        pltpu.VMEM((bsz * MLSTM_HEADS, 8, LANES), F32)],
        compiler_params=_params(1),
        name="mlstm",
    )(mqt, mk, mvt, mo, g, gt, g_mh.reshape(1, MLSTM_HEADS * MLSTM_V))


def kernel(x, p, positions, g_mix, g_ffn, g_ple, ev_w_in, ev_w_conv, ev_g_v, ev_w_s, ev_b_s, ev_w_out,
           od_w_in, od_b_gate, od_g_qa, od_g_kva, od_w_q_up, od_w_kv_up, od_g_q, od_g_k, od_g_mh, od_w_out,
           w_gate, w_up, w_down, w_ple_proj, w_ple_gate):
    h = x
    depth = g_mix.shape[0]
    ffn_w = (w_gate, w_up, w_down, w_ple_gate, w_ple_proj)
    for layer in range(depth):
        j = layer // 2
        if layer % 2 == 0:
            y1, y2, cast = _even_mixer(h, g_mix[layer], ev_w_in[j], ev_w_conv[j], ev_g_v[j], ev_w_s[j], ev_b_s[j],
                                       ffn_w if layer == 0 else ())
            ffn_w = cast if layer == 0 else ffn_w
            w_out = ev_w_out[j]
        else:
            qt, k, vt, spread, mqt, mk, mvt, mo, g, gt = _odd_proj(
                h, positions, g_mix[layer], od_w_in[j], od_b_gate[j], od_g_qa[j], od_g_kva[j],
                od_w_q_up[j], od_w_kv_up[j], od_g_q[j], od_g_k[j])
            y1 = lax.cond(jnp.max(spread) <= MAX_SCORE_SPREAD,
                          functools.partial(_attention, running_max=False),
                          functools.partial(_attention, running_max=True), qt, k, vt)
            y2 = _mlstm(mqt, mk, mvt, mo, g, gt, od_g_mh[j])
            w_out = od_w_out[j]
        h = _ffn_ple(layer, h, y1, y2, p, w_out, g_ffn[layer], g_ple[layer], *ffn_w)
    return h
```

```python
import functools
import math

import numpy as np
import jax
import jax.numpy as jnp
from jax import lax
from jax.experimental import pallas as pl
from jax.experimental.pallas import tpu as pltpu

F32 = jnp.float32
BF16 = jnp.bfloat16

D_MODEL = 1024
D_PLE = 256
EPS = 1e-6

CONV_DIM = 512
GMLP_DIM = 512
GMLP_HEADS = 8
GMLP_HEAD_DIM = 64
GMLP_CHUNK = 128
EVEN_IN = 3 * CONV_DIM + 2 * GMLP_DIM

MLA_HEADS = 8
MLA_NOPE = 64
MLA_ROPE = 32
MLA_V = 64
MLA_QK = MLA_NOPE + MLA_ROPE
Q_LORA = 384
KV_LORA = 256
ROPE_THETA = 10000.0
MLSTM_HEADS = 4
MLSTM_QK = 64
MLSTM_V = 128
D_FF = 2816

LANES = 128
MXU_EDGE = 256

TM = 512
TM_PROJ = 1024
TM_ODD = 1024
TM_FFN = 512
TQ = 512
TK = 512
KSUB = MXU_EDGE
LOOKAHEAD = 2
KV_UNROLL = 4
ML = 256
V_EXT = MLSTM_V + 16
HEAD_PAD = LANES
ODD_IN_PAD = 2304
TAIL_OFF = ODD_IN_PAD - LANES
FF_CHUNKS = ((0, 1024), (1024, 2048), (2048, 2816))
MAX_ABS_SCORE = 100.0
SCORE_SCALE = MLA_QK ** -0.5 * math.log2(math.e)
BF16_SLACK = 1.01

VMEM_LIMIT = 56 * 1024 * 1024


def _dot(a, b):
    return jnp.dot(a, b, preferred_element_type=F32)


def _dot_nt(a, b):
    return lax.dot_general(a, b, (((1,), (1,)), ((), ())), preferred_element_type=F32)


def _rms_rows(x, g):
    ms = jnp.mean(x * x, axis=-1, keepdims=True)
    return x * lax.rsqrt(ms + EPS) * g


def _rms_cols(x, g):
    ms = jnp.mean(x * x, axis=0, keepdims=True)
    return x * lax.rsqrt(ms + EPS) * g


def _sigmoid(x):
    return 0.5 * jnp.tanh(0.5 * x) + 0.5


def _const_spec(shape):
    zeros = (0,) * len(shape)
    return pl.BlockSpec(shape, lambda *_: zeros, pipeline_mode=pl.Buffered(1))


def _params(n_axes):
    return pltpu.CompilerParams(dimension_semantics=("arbitrary",) * n_axes,
                                vmem_limit_bytes=VMEM_LIMIT)


def _even_kernel(h_ref, gmix_ref, win_ref, wconv_ref, gmat_ref, gv_ref, wpair_ref, bm_ref, *rest, n_cast):
    cast_in, cast_out = rest[:n_cast], rest[n_cast + 2:2 * n_cast + 2]
    ya_ref, yb_ref = rest[n_cast:n_cast + 2]
    zbuf_ref = rest[-1]
    for src_ref, dst_ref in zip(cast_in, cast_out):
        dst_ref[...] = src_ref[...].astype(BF16)

    @pl.when(pl.program_id(1) == 0)
    def _():
        zbuf_ref[0:8, :] = jnp.zeros((8, CONV_DIM), F32)

    lane = lax.broadcasted_iota(jnp.int32, (GMLP_CHUNK, LANES), 1)
    row_t = lax.broadcasted_iota(jnp.int32, (GMLP_CHUNK, 2 * GMLP_CHUNK), 0)
    col_s = lax.broadcasted_iota(jnp.int32, (GMLP_CHUNK, 2 * GMLP_CHUNK), 1) % GMLP_CHUNK
    tril = col_s <= row_t
    wms = [jnp.where(tril, wpair_ref[j], 0.0).astype(BF16) for j in range(GMLP_HEADS // 2)]
    col = lambda i: win_ref[:, i * CONV_DIM:(i + 1) * CONV_DIM]
    wc = wconv_ref[...]

    for s0 in range(0, TM_PROJ, TM):
        hn = _rms_rows(h_ref[0, s0:s0 + TM], gmix_ref[...]).astype(BF16)
        v = _dot(hn, col(4))
        u = _dot(hn, col(3))
        z_cx = _dot(hn, win_ref[:, CONV_DIM:3 * CONV_DIM])
        gv = jax.nn.gelu(v)
        ss = _dot((gv * gv).astype(BF16), gmat_ref[...])
        b_gate = _dot(hn, col(0))

        zz = z_cx[:, 0:CONV_DIM] * z_cx[:, CONV_DIM:2 * CONV_DIM]
        zbuf_ref[8 + s0:8 + s0 + TM, :] = zz
        z1 = zbuf_ref[7 + s0:7 + s0 + TM, :]
        z2 = zbuf_ref[6 + s0:6 + s0 + TM, :]
        conv = wc[2:3] * zz + wc[1:2] * z1 + wc[0:1] * z2
        ya_ref[0, s0:s0 + TM] = (b_gate * conv).astype(BF16)
        if s0 + TM == TM_PROJ:
            zbuf_ref[0:8, :] = zz[TM - 8:TM, :]

        gu = jax.nn.gelu(u)
        vn = gv * lax.rsqrt(ss * (1.0 / GMLP_HEAD_DIM) + EPS) * gv_ref[...]
        for c in range(TM // GMLP_CHUNK):
            r0 = c * GMLP_CHUNK
            outs = []
            for j in range(GMLP_HEADS // 2):
                vp = vn[r0:r0 + GMLP_CHUNK, j * LANES:(j + 1) * LANES]
                lo = jnp.where(lane < GMLP_HEAD_DIM, vp, 0.0).astype(BF16)
                hi = jnp.where(lane >= GMLP_HEAD_DIM, vp, 0.0).astype(BF16)
                outs.append(_dot(wms[j], jnp.concatenate([lo, hi], axis=0)))
            mixed = jnp.concatenate(outs, axis=1) + bm_ref[...]
            yb_ref[0, s0 + r0:s0 + r0 + GMLP_CHUNK, :] = (gu[r0:r0 + GMLP_CHUNK] * mixed).astype(BF16)


def _even_mixer(h, g_mix, w_in, w_conv, g_v, w_s, b_s, stacked_weights):
    bsz, seq, _ = h.shape
    steps = bsz * (seq // TM_PROJ)
    slab = lambda w: pl.BlockSpec((w.shape[0], w.shape[1] // steps, w.shape[2]),
                                  lambda b, i: (0, b * (seq // TM_PROJ) + i, 0))
    for w in stacked_weights:
        assert w.shape[1] % (16 * steps) == 0, "slab rows must fill bf16 (16, 128) tiles"
    gmat = jnp.asarray(np.kron(np.eye(GMLP_HEADS), np.ones((GMLP_HEAD_DIM, GMLP_HEAD_DIM))), BF16)
    wpair = w_s.reshape(GMLP_HEADS // 2, 2, GMLP_CHUNK, GMLP_CHUNK).transpose(0, 2, 1, 3)
    wpair = wpair.reshape(GMLP_HEADS // 2, GMLP_CHUNK, 2 * GMLP_CHUNK)
    bm = jnp.repeat(b_s.T, GMLP_HEAD_DIM, axis=1)
    out_sds = jax.ShapeDtypeStruct((bsz, seq, CONV_DIM), BF16)
    tile = lambda w: pl.BlockSpec((1, TM_PROJ, w), lambda b, i: (b, i, 0))
    outs = pl.pallas_call(
        functools.partial(_even_kernel, n_cast=len(stacked_weights)),
        grid=(bsz, seq // TM_PROJ),
        in_specs=[tile(D_MODEL), _const_spec((1, D_MODEL)), _const_spec((D_MODEL, EVEN_IN)),
                  _const_spec((3, CONV_DIM)), _const_spec((GMLP_DIM, GMLP_DIM)), _const_spec((1, GMLP_DIM)),
                  _const_spec((GMLP_HEADS // 2, GMLP_CHUNK, 2 * GMLP_CHUNK)),
                  _const_spec((GMLP_CHUNK, GMLP_DIM))] + [slab(w) for w in stacked_weights],
        out_specs=[tile(CONV_DIM), tile(GMLP_DIM)] + [slab(w) for w in stacked_weights],
        out_shape=[out_sds, out_sds] + [jax.ShapeDtypeStruct(w.shape, BF16) for w in stacked_weights],
        scratch_shapes=[pltpu.VMEM((TM_PROJ + 8, CONV_DIM), F32)],
        compiler_params=_params(2),
        name="even_mixer",
    )(h, g_mix.reshape(1, D_MODEL), w_in.astype(BF16), w_conv, gmat, g_v.reshape(1, GMLP_DIM), wpair, bm,
      *stacked_weights)
    return outs[0], outs[1], tuple(outs[2:])


def _ffn_kernel(h_ref, y1_ref, y2_ref, p_ref, wout_ref, gffn_ref, wg_ref, wu_ref, wd_ref,
                gple_ref, wpg_ref, wpp_ref, o_ref):
    half = D_MODEL // 2
    for s0 in range(0, TM_FFN, TM):
        rows = [slice(s0 + r, s0 + r + TM // 2) for r in (0, TM // 2)]
        p_b = p_ref[0, 0, s0:s0 + TM].astype(BF16)
        xs = [h_ref[0, r] + _dot(jnp.concatenate([y1_ref[0, r], y2_ref[0, r]], axis=-1), wout_ref[...])
              for r in rows]
        pp_lo = _dot(p_b, wpp_ref[:, :half])
        x = jnp.concatenate(xs, axis=0)
        hn = jnp.concatenate([_rms_rows(xr, gffn_ref[...]).astype(BF16) for xr in xs], axis=0)
        acc = x
        for lo, hi in FF_CHUNKS:
            g = _dot(hn, wg_ref[:, lo:hi])
            u = _dot(hn, wu_ref[:, lo:hi])
            a = (g * _sigmoid(g) * u).astype(BF16)
            acc = acc + _dot(a, wd_ref[lo:hi, :])
        pp_hi = _dot(p_b, wpp_ref[:, half:])
        pp = jnp.concatenate([pp_lo, pp_hi], axis=-1)
        for i, r in enumerate(rows):
            sub = slice(i * (TM // 2), (i + 1) * (TM // 2))
            hn2 = _rms_rows(acc[sub], gple_ref[...]).astype(BF16)
            o_ref[0, r] = acc[sub] + _sigmoid(_dot(hn2, wpg_ref[...])) * pp[sub]


def _ffn_ple(layer, h, y1, y2, p, w_out, g_ffn, g_ple, w_gate, w_up, w_down, w_pg, w_pp):
    bsz, seq, _ = h.shape
    tile = lambda w: pl.BlockSpec((1, TM_FFN, w), lambda b, i: (b, i, 0))
    layer_spec = lambda r, c: pl.BlockSpec((None, r, c), lambda *_: (layer, 0, 0), pipeline_mode=pl.Buffered(1))
    return pl.pallas_call(
        _ffn_kernel,
        grid=(bsz, seq // TM_FFN),
        in_specs=[tile(D_MODEL), tile(512), tile(512),
                  pl.BlockSpec((1, 1, TM_FFN, D_PLE), lambda b, i: (layer, b, i, 0)),
                  _const_spec((D_MODEL, D_MODEL)), _const_spec((1, D_MODEL)),
                  layer_spec(D_MODEL, D_FF), layer_spec(D_MODEL, D_FF), layer_spec(D_FF, D_MODEL),
                  _const_spec((1, D_MODEL)), layer_spec(D_MODEL, D_MODEL), layer_spec(D_PLE, D_MODEL)],
        out_specs=tile(D_MODEL),
        out_shape=jax.ShapeDtypeStruct((bsz, seq, D_MODEL), F32),
        compiler_params=_params(2),
        name="ffn_ple",
    )(h, y1, y2, p, w_out.astype(BF16), g_ffn.reshape(1, D_MODEL), w_gate, w_up, w_down,
      g_ple.reshape(1, D_MODEL), w_pg, w_pp)


def _log_sigmoid(x):
    return jnp.minimum(x, 0.0) - jnp.log1p(jnp.exp(-jnp.abs(x)))


def _rope_cols(xr, cos, sin):
    half = MLA_ROPE // 2
    x1, x2 = xr[:half], xr[half:]
    return x1 * cos - x2 * sin, x2 * cos + x1 * sin


def _join_kernel(wlat_ref, wml_ref, wtail_ref, win_ref):
    win_ref[:, 0:Q_LORA + KV_LORA] = wlat_ref[...]
    win_ref[:, Q_LORA + KV_LORA:TAIL_OFF] = wml_ref[...]
    win_ref[:, TAIL_OFF:ODD_IN_PAD] = wtail_ref[...]


def _odd_kernel(h_ref, pos_ref, gmix_ref, win_ref, bias_ref, gqa_ref, gkva_ref, wqt_ref,
                wkvt_ref, gqn_ref, gqr_ref, gkn_ref, gkr_ref, freq_ref,
                qt_ref, k_ref, vt_ref, mqt_ref, mk_ref, mvt_ref, mo_ref, g_ref, gt_ref):
    for sub in range(TM_ODD // TM):
        _odd_sub_tile(sub, h_ref, pos_ref, gmix_ref, win_ref, bias_ref, gqa_ref, gkva_ref, wqt_ref, wkvt_ref,
                      gqn_ref, gqr_ref, gkn_ref, gkr_ref, freq_ref,
                      qt_ref, k_ref, vt_ref, mqt_ref, mk_ref, mvt_ref, mo_ref, g_ref, gt_ref)


def _odd_sub_tile(sub, h_ref, pos_ref, gmix_ref, win_ref, bias_ref, gqa_ref, gkva_ref, wqt_ref, wkvt_ref,
                  gqn_ref, gqr_ref, gkn_ref, gkr_ref, freq_ref,
                  qt_ref, k_ref, vt_ref, mqt_ref, mk_ref, mvt_ref, mo_ref, g_ref, gt_ref):
    tok = slice(sub * TM, (sub + 1) * TM)
    hn = _rms_rows(h_ref[0, tok], gmix_ref[...]).astype(BF16)
    z = _dot(hn, win_ref[...])
    q_lat = z[:, 0:384]
    kv_lat = z[:, 384:640]

    mqt_ref[0, :, tok] = (z[:, 640:896] * (MLSTM_QK ** -0.5)).T.astype(BF16)
    mk_ref[0, tok] = z[:, 896:1152].astype(BF16)
    mvt_ref[0, :, tok] = z[:, 1152:1664].T.astype(BF16)
    mo_ref[0, tok] = z[:, 1664:2176]
    tail = z[:, TAIL_OFF:ODD_IN_PAD] + bias_ref[...]
    lane = lax.broadcasted_iota(jnp.int32, (TM, LANES), 1)
    f_lane = (lane >= MLA_ROPE + MLSTM_HEADS) & (lane < MLA_ROPE + 2 * MLSTM_HEADS)
    tail = jnp.where(f_lane, _log_sigmoid(tail), tail)
    g_ref[0, tok] = tail
    tail_t = tail.T
    gt_ref[0, :, tok] = tail_t[MLA_ROPE:MLA_ROPE + 2 * MLSTM_HEADS]

    qn = _rms_rows(q_lat, gqa_ref[...]).astype(BF16)
    kvn = _rms_rows(kv_lat, gkva_ref[...]).astype(BF16)
    q_t = _dot_nt(wqt_ref[...], qn)
    kv_t = _dot_nt(wkvt_ref[...], kvn)

    ang = freq_ref[...] * pos_ref[0, :, tok].astype(F32)
    cos = jnp.cos(ang)
    sin = jnp.sin(ang)
    kr1, kr2 = _rope_cols(_rms_cols(tail_t[0:MLA_ROPE], gkr_ref[...]), cos, sin)

    pad = jnp.zeros((HEAD_PAD - MLA_QK, TM), F32)
    for hd in range(MLA_HEADS):
        q0 = hd * MLA_QK
        q_nope = _rms_cols(q_t[q0:q0 + MLA_NOPE], gqn_ref[...])
        qr1, qr2 = _rope_cols(_rms_cols(q_t[q0 + MLA_NOPE:q0 + MLA_QK], gqr_ref[...]), cos, sin)
        qt_ref[0, hd, :, tok] = jnp.concatenate([q_nope, qr1, qr2, pad], axis=0).astype(BF16)
        k0 = hd * (MLA_NOPE + MLA_V)
        k_nope = _rms_cols(kv_t[k0:k0 + MLA_NOPE], gkn_ref[...])
        k_ref[0, hd, tok] = jnp.concatenate([k_nope, kr1, kr2, pad], axis=0).T.astype(BF16)
        vt_ref[0, sub, hd] = kv_t[k0 + MLA_NOPE:k0 + MLA_NOPE + MLA_V].astype(BF16)


def _odd_proj(h, positions, g_mix, w_in, b_gate, g_qa, g_kva, w_q_up, w_kv_up, g_q, g_k):
    bsz, seq, _ = h.shape
    c = np.cumsum([0, Q_LORA + KV_LORA, MLA_ROPE, 2 * MLSTM_HEADS * (MLSTM_QK + MLSTM_V), 2 * MLSTM_HEADS])
    seg = lambda i: w_in[:, c[i]:c[i + 1]].astype(BF16)
    tail_pad = jnp.zeros((D_MODEL, LANES - MLA_ROPE - 2 * MLSTM_HEADS), BF16)
    groups = (seg(0), seg(2), jnp.concatenate([seg(1), seg(3), tail_pad], axis=1))
    w_in_r = pl.pallas_call(
        _join_kernel,
        out_shape=jax.ShapeDtypeStruct((D_MODEL, ODD_IN_PAD), BF16),
        compiler_params=pltpu.CompilerParams(vmem_limit_bytes=VMEM_LIMIT),
        name="odd_w_in_join",
    )(*groups)
    bias = jnp.zeros((1, LANES), F32).at[0, MLA_ROPE:MLA_ROPE + 2 * MLSTM_HEADS].set(b_gate)
    inv_freq = ROPE_THETA ** (-jnp.arange(0, MLA_ROPE, 2, dtype=F32) / MLA_ROPE)
    col = lambda a: a.reshape(-1, 1)
    tile3 = lambda w: pl.BlockSpec((1, TM_ODD, w), lambda b, i: (b, i, 0))
    tile3t = lambda r: pl.BlockSpec((1, r, TM_ODD), lambda b, i: (b, 0, i))
    out_shape = [
        jax.ShapeDtypeStruct((bsz, MLA_HEADS, HEAD_PAD, seq), BF16),
        jax.ShapeDtypeStruct((bsz, MLA_HEADS, seq, HEAD_PAD), BF16),
        jax.ShapeDtypeStruct((bsz, seq // TK, MLA_HEADS, MLA_V, TK), BF16),
        jax.ShapeDtypeStruct((bsz, 256, seq), BF16),
        jax.ShapeDtypeStruct((bsz, seq, 256), BF16),
        jax.ShapeDtypeStruct((bsz, 512, seq), BF16),
        jax.ShapeDtypeStruct((bsz, seq, 512), F32),
        jax.ShapeDtypeStruct((bsz, seq, LANES), F32),
        jax.ShapeDtypeStruct((bsz, 2 * MLSTM_HEADS, seq), F32),
    ]
    out_specs = [
        pl.BlockSpec((1, MLA_HEADS, HEAD_PAD, TM_ODD), lambda b, i: (b, 0, 0, i)),
        pl.BlockSpec((1, MLA_HEADS, TM_ODD, HEAD_PAD), lambda b, i: (b, 0, i, 0)),
        pl.BlockSpec((1, TM_ODD // TK, MLA_HEADS, MLA_V, TK), lambda b, i: (b, i, 0, 0, 0)),
        tile3t(256), tile3(256), tile3t(512), tile3(512), tile3(LANES),
        tile3t(2 * MLSTM_HEADS),
    ]
    assert TM == TK, "one v^T key tile is written per sub-tile"
    return pl.pallas_call(
        _odd_kernel,
        grid=(bsz, seq // TM_ODD),
        in_specs=[tile3(D_MODEL), tile3t(1),
                  _const_spec((1, D_MODEL)), _const_spec((D_MODEL, ODD_IN_PAD)), _const_spec((1, LANES)),
                  _const_spec((1, Q_LORA)), _const_spec((1, KV_LORA)),
                  _const_spec((MLA_HEADS * MLA_QK, Q_LORA)), _const_spec((MLA_HEADS * (MLA_NOPE + MLA_V), KV_LORA)),
                  _const_spec((MLA_NOPE, 1)), _const_spec((MLA_ROPE, 1)),
                  _const_spec((MLA_NOPE, 1)), _const_spec((MLA_ROPE, 1)), _const_spec((MLA_ROPE // 2, 1))],
        out_specs=out_specs,
        out_shape=out_shape,
        compiler_params=_params(2),
        name="odd_proj",
    )(h, positions.reshape(bsz, 1, seq), g_mix.reshape(1, D_MODEL), w_in_r, bias,
      g_qa.reshape(1, Q_LORA), g_kva.reshape(1, KV_LORA), w_q_up.T.astype(BF16), w_kv_up.T.astype(BF16),
      col(g_q[:MLA_NOPE] * SCORE_SCALE), col(g_q[MLA_NOPE:] * SCORE_SCALE),
      col(g_k[:MLA_NOPE]), col(g_k[MLA_NOPE:]),
      inv_freq.reshape(-1, 1))


def _attn_kernel(qt_ref, k_ref, vt_ref, o_ref, m_ref, l_ref, acc_ref, kbuf_ref, vbuf_ref, *, running_max):
    qi = pl.program_id(1)
    m_ref[...] = jnp.full(m_ref.shape, -jnp.inf, F32)
    l_ref[...] = jnp.zeros(l_ref.shape, F32)
    acc_ref[...] = jnp.zeros(acc_ref.shape, F32)
    kbuf_ref[qi] = k_ref[0]
    vbuf_ref[qi] = vt_ref[0, 0]

    def scores(unit):
        j, masked, hd, part = unit
        t0 = part * KSUB if masked else 0
        k = kbuf_ref[j, hd, part * KSUB:(part + 1) * KSUB, :]
        return _dot(k, qt_ref[0, hd, :, t0:])

    def kv_tiles(*tiles):
        units = [(j, masked, hd, part) for j, masked in tiles
                 for hd in range(MLA_HEADS) for part in range(TK // KSUB)]
        pending = [scores(u) for u in units[:LOOKAHEAD]]
        for i, (j, masked, hd, part) in enumerate(units):
            s = pending.pop(0)
            if i + LOOKAHEAD < len(units):
                pending.append(scores(units[i + LOOKAHEAD]))
            t0 = part * KSUB if masked else 0
            if masked:
                row_s = lax.broadcasted_iota(jnp.int32, s.shape, 0) + part * KSUB
                col_t = lax.broadcasted_iota(jnp.int32, s.shape, 1) + t0
                s = jnp.where(row_s <= col_t, s, -jnp.inf)
            v_t = vbuf_ref[j, hd, :, part * KSUB:(part + 1) * KSUB]
            if running_max:
                m_old = m_ref[hd, :, t0:]
                m_new = jnp.maximum(m_old, jnp.max(s, axis=0, keepdims=True))
                alpha = jnp.exp2(m_old - m_new)
                p = jnp.exp2(s - m_new)
                m_ref[hd, :, t0:] = m_new
                l_ref[hd, :, t0:] = alpha * l_ref[hd, :, t0:] + jnp.sum(p, axis=0, keepdims=True)
                acc_ref[hd, :, t0:] = alpha * acc_ref[hd, :, t0:] + _dot(v_t, p.astype(BF16))
            else:
                p = jnp.exp2(s)
                l_ref[hd, :, t0:] += jnp.sum(p, axis=0, keepdims=True)
                acc_ref[hd, :, t0:] += _dot(v_t, p.astype(BF16))

    unroll = 1 if running_max else KV_UNROLL

    def group(i, _):
        kv_tiles(*[(unroll * i + t, False) for t in range(unroll)])
        return 0

    lax.fori_loop(0, qi // unroll, group, 0)
    for left in range(unroll):
        @pl.when(qi % unroll == left)
        def _():
            kv_tiles(*[(qi - left + t, False) for t in range(left)], (qi, True))

    out_t = jnp.concatenate([acc_ref[hd] / l_ref[hd] for hd in range(MLA_HEADS)], axis=0)
    o_ref[0] = out_t.T.astype(BF16)


def _attention(qt, k, vt, *, running_max):
    bsz, _, _, seq = qt.shape
    return pl.pallas_call(
        functools.partial(_attn_kernel, running_max=running_max),
        grid=(bsz, seq // TQ),
        in_specs=[pl.BlockSpec((1, MLA_HEADS, HEAD_PAD, TQ), lambda b, i: (b, 0, 0, i)),
                  pl.BlockSpec((1, MLA_HEADS, TK, HEAD_PAD), lambda b, i: (b, 0, i, 0)),
                  pl.BlockSpec((1, 1, MLA_HEADS, MLA_V, TK), lambda b, i: (b, i, 0, 0, 0))],
        out_specs=pl.BlockSpec((1, TQ, MLA_HEADS * MLA_V), lambda b, i: (b, i, 0)),
        out_shape=jax.ShapeDtypeStruct((bsz, seq, MLA_HEADS * MLA_V), BF16),
        scratch_shapes=[pltpu.VMEM((MLA_HEADS, 1, TQ), F32), pltpu.VMEM((MLA_HEADS, 1, TQ), F32),
                        pltpu.VMEM((MLA_HEADS, MLA_V, TQ), F32),
                        pltpu.VMEM((seq // TK, MLA_HEADS, TK, HEAD_PAD), BF16),
                        pltpu.VMEM((seq // TK, MLA_HEADS, MLA_V, TK), BF16)],
        compiler_params=_params(2),
        name="attention_running_max" if running_max else "attention",
    )(qt, k, vt)


def _split3(x):
    hi = x.astype(BF16)
    r = x - hi.astype(F32)
    mid = r.astype(BF16)
    lo = (r - mid.astype(F32)).astype(BF16)
    return hi, mid, lo


def _mlstm_kernel(mqt_ref, mk_ref, mvt_ref, mo_ref, g_ref, gt_ref, gnext_ref, gtnext_ref, gmh_ref, yd_ref,
                  ct_ref, m_ref, acol_ref, arow_ref):
    bsz = mqt_ref.shape[0]

    row_s = lax.broadcasted_iota(jnp.int32, (ML, ML), 0)
    col_t = lax.broadcasted_iota(jnp.int32, (ML, ML), 1)
    visible = row_s <= col_t
    upper = jnp.where(visible, 1.0, 0.0).astype(BF16)
    lower = jnp.where(col_t <= row_s, 1.0, 0.0).astype(BF16)
    lane = lax.broadcasted_iota(jnp.int32, (ML, LANES), 1)
    k_lo = lane < MLSTM_QK
    ones_rows = jnp.where(lax.broadcasted_iota(jnp.int32, (V_EXT - MLSTM_V, ML), 0) == 0, 1.0, 0.0)
    i_lane0 = MLA_ROPE

    def store_cumsums(gcol_ref, grow_ref):
        for b in range(bsz):
            f_al = pltpu.roll(gcol_ref[b], LANES - MLSTM_HEADS, axis=1)
            acol_ref[b] = sum(_dot(lower, piece) for piece in _split3(f_al))
            grow = grow_ref[b]
            arow_ref[b] = sum(_dot(piece, upper) for piece in _split3(jnp.concatenate([grow, grow], axis=0)))

    @pl.when(pl.program_id(0) == 0)
    def _():
        ct_ref[...] = jnp.zeros(ct_ref.shape, F32)
        m_ref[...] = jnp.zeros(m_ref.shape, F32)
        store_cumsums(g_ref, gt_ref)

    chains = [(b, hd) for b in range(bsz) for hd in range(MLSTM_HEADS)]
    a_row, i_row, c_col = {}, {}, {}
    for b in range(bsz):
        gcol = g_ref[b]
        grow = gt_ref[b]
        a_rows = arow_ref[b]
        c_all = gcol - acol_ref[b]
        for hd in range(MLSTM_HEADS):
            a_row[b, hd] = a_rows[MLSTM_HEADS + hd:MLSTM_HEADS + hd + 1]
            i_row[b, hd] = grow[hd:hd + 1]
            c_col[b, hd] = c_all[:, i_lane0 + hd:i_lane0 + hd + 1]

    k_m, qt_pair, vt_ext, sqk, qc = {}, {}, {}, {}, {}
    for b, hd in chains:
        pair, half = hd // 2, hd % 2
        k_pair = mk_ref[b, :, pair * LANES:(pair + 1) * LANES]
        k_m[b, hd] = jnp.where(k_lo if half == 0 else ~k_lo, k_pair.astype(F32), 0.0).astype(BF16)
        qt_pair[b, hd] = mqt_ref[b, pair * LANES:(pair + 1) * LANES, :]
        sqk[b, hd] = _dot(k_m[b, hd], qt_pair[b, hd])
    for b, hd in chains:
        idx = b * MLSTM_HEADS + hd
        qc[b, hd] = _dot(ct_ref[idx].astype(BF16), qt_pair[b, hd])

    m_t, m_new, m_prev, w_inter, s_t, den_intra = {}, {}, {}, {}, {}, {}
    for b, hd in chains:
        idx = b * MLSTM_HEADS + hd
        m_prev[b, hd] = m_ref[idx][0:1, 0:1]
        d_t = jnp.where(visible, a_row[b, hd] + c_col[b, hd], -jnp.inf)
        inter = a_row[b, hd] + m_prev[b, hd]
        m_t[b, hd] = jnp.maximum(inter, jnp.max(d_t, axis=0, keepdims=True))
        w_inter[b, hd] = jnp.exp(inter - m_t[b, hd])
        s_t[b, hd] = sqk[b, hd] * jnp.exp(d_t - m_t[b, hd])
        den_intra[b, hd] = jnp.sum(s_t[b, hd], axis=0, keepdims=True)

    nv = {}
    for b, hd in chains:
        v_t = mvt_ref[b, hd * MLSTM_V:(hd + 1) * MLSTM_V, :]
        vt_ext[b, hd] = jnp.concatenate([v_t.astype(F32), ones_rows], axis=0)
        nv[b, hd] = _dot(vt_ext[b, hd].astype(BF16), s_t[b, hd].astype(BF16))

    for b, hd in chains:
        idx = b * MLSTM_HEADS + hd
        a_last = a_row[b, hd][:, ML - 1:ML]
        m_new[b, hd] = m_t[b, hd][:, ML - 1:ML]
        w_s = jnp.exp(a_last - a_row[b, hd] + i_row[b, hd] - m_new[b, hd])
        decay = jnp.exp(a_last + m_prev[b, hd] - m_new[b, hd])
        upd = _dot((vt_ext[b, hd] * w_s).astype(BF16), k_m[b, hd])
        ct_ref[idx] = decay * ct_ref[idx] + upd
        m_ref[idx] = jnp.broadcast_to(m_new[b, hd], (8, LANES))

    for b in range(bsz):
        h_t = []
        for hd in range(MLSTM_HEADS):
            num = nv[b, hd][:MLSTM_V] + w_inter[b, hd] * qc[b, hd][:MLSTM_V]
            den = den_intra[b, hd] + w_inter[b, hd] * qc[b, hd][MLSTM_V:MLSTM_V + 1]
            hh = num / jnp.maximum(jnp.abs(den), jnp.exp(-m_t[b, hd]))
            h_t.append(hh * lax.rsqrt(jnp.mean(hh * hh, axis=0, keepdims=True) + EPS))
        hn = jnp.concatenate(h_t, axis=0).T * gmh_ref[...]
        yd_ref[b] = (_sigmoid(mo_ref[b]) * hn).astype(BF16)

    store_cumsums(gnext_ref, gtnext_ref)


def _mlstm(mqt, mk, mvt, mo, g, gt, g_mh):
    bsz, seq, _ = mk.shape
    last = seq // ML - 1
    tile3 = lambda w: pl.BlockSpec((bsz, ML, w), lambda i: (0, i, 0))
    tile3t = lambda r: pl.BlockSpec((bsz, r, ML), lambda i: (0, 0, i))
    return pl.pallas_call(
        _mlstm_kernel,
        grid=(seq // ML,),
        in_specs=[tile3t(256), tile3(256), tile3t(512), tile3(512),
                  tile3(LANES), tile3t(2 * MLSTM_HEADS),
                  pl.BlockSpec((bsz, ML, LANES), lambda i: (0, jnp.minimum(i + 1, last), 0)),
                  pl.BlockSpec((bsz, 2 * MLSTM_HEADS, ML), lambda i: (0, 0, jnp.minimum(i + 1, last))),
                  _const_spec((1, MLSTM_HEADS * MLSTM_V))],
        out_specs=tile3(512),
        out_shape=jax.ShapeDtypeStruct((bsz, seq, MLSTM_HEADS * MLSTM_V), BF16),
        scratch_shapes=[pltpu.VMEM((bsz * MLSTM_HEADS, V_EXT, 2 * MLSTM_QK), F32),
                        pltpu.VMEM((bsz * MLSTM_HEADS, 8, LANES), F32),
                        pltpu.VMEM((bsz, ML, LANES), F32), pltpu.VMEM((bsz, 16, ML), F32)],
        compiler_params=_params(1),
        name="mlstm",
    )(mqt, mk, mvt, mo, g, gt, g, gt, g_mh.reshape(1, MLSTM_HEADS * MLSTM_V))


def kernel(x, p, positions, g_mix, g_ffn, g_ple, ev_w_in, ev_w_conv, ev_g_v, ev_w_s, ev_b_s, ev_w_out,
           od_w_in, od_b_gate, od_g_qa, od_g_kva, od_w_q_up, od_w_kv_up, od_g_q, od_g_k, od_g_mh, od_w_out,
           w_gate, w_up, w_down, w_ple_proj, w_ple_gate):
    h = x
    depth = g_mix.shape[0]
    ffn_w = (w_gate, w_up, w_down, w_ple_gate, w_ple_proj)
    for layer in range(depth):
        j = layer // 2
        if layer % 2 == 0:
            y1, y2, cast = _even_mixer(h, g_mix[layer], ev_w_in[j], ev_w_conv[j], ev_g_v[j], ev_w_s[j], ev_b_s[j],
                                       ffn_w if layer == 0 else ())
            ffn_w = cast if layer == 0 else ffn_w
            w_out = ev_w_out[j]
        else:
            qt, k, vt, mqt, mk, mvt, mo, g, gt = _odd_proj(
                h, positions, g_mix[layer], od_w_in[j], od_b_gate[j], od_g_qa[j], od_g_kva[j],
                od_w_q_up[j], od_w_kv_up[j], od_g_q[j], od_g_k[j])
            norm_bound = lambda g: jnp.sqrt(MLA_NOPE * jnp.max(g[:MLA_NOPE] ** 2)
                                            + MLA_ROPE * jnp.max(g[MLA_NOPE:] ** 2))
            score_bound = BF16_SLACK * SCORE_SCALE * norm_bound(od_g_q[j]) * norm_bound(od_g_k[j])
            y1 = lax.cond(score_bound <= MAX_ABS_SCORE,
                          functools.partial(_attention, running_max=False),
                          functools.partial(_attention, running_max=True), qt, k, vt)
            y2 = _mlstm(mqt, mk, mvt, mo, g, gt, od_g_mh[j])
            w_out = od_w_out[j]
        h = _ffn_ple(layer, h, y1, y2, p, w_out, g_ffn[layer], g_ple[layer], *ffn_w)
    return h
```
